```python
import math
import jax, jax.numpy as jnp
from jax import lax
import numpy as np

D_MODEL = 1024
BATCH = 4
SEQ = 4096
DEPTH = 2

GRID_W = 64
CTX_LEN = 256
N_MIXERS = 2
MIXER_GDN = 0
MIXER_POOL = 1
N_DIRS = 2

GDN_HEADS = 8
GDN_HEAD_DIM = 128
GDN_KEY_DIM = GDN_HEADS * GDN_HEAD_DIM
GDN_VALUE_DIM = GDN_HEADS * GDN_HEAD_DIM
GDN_QKV_DIM = 2 * GDN_KEY_DIM + GDN_VALUE_DIM
GDN_AB_DIM = N_DIRS * 2 * GDN_HEADS
GDN_PROJ_DIM = GDN_QKV_DIM + GDN_VALUE_DIM + GDN_AB_DIM
CONV_W = 5
CHUNK = 64

POOL_WINDOWS = (2, 4, 8, 16)
POOL_GROUPS = len(POOL_WINDOWS)
POOL_GROUP_DIM = D_MODEL // POOL_GROUPS

N_EXPERTS = 16
EC_CAPACITY_FACTOR = 2
D_EXPERT = 2048

N_MOD = 6
RMS_EPS = 1e-6

kernel_name = 'bidir_gdn_pool_ec_moe_dit'


def rmsnorm(x, g):
    xf = x.astype(jnp.float32)
    y = xf * lax.rsqrt(jnp.mean(xf * xf, axis=-1, keepdims=True) + RMS_EPS)
    return (y * g.astype(jnp.float32)).astype(x.dtype)


def l2norm(x):
    xf = x.astype(jnp.float32)
    return xf * lax.rsqrt(jnp.sum(xf * xf, axis=-1, keepdims=True) + RMS_EPS)


def modulate(xn, shift, scale):
    return xn * (1 + scale) + shift


def short_conv(x, w):
    pad = CONV_W // 2
    return lax.conv_general_dilated(x, w[:, None, :], window_strides=(1,), padding=[(pad, pad)],
                                    dimension_numbers=('NWC', 'WIO', 'NWC'),
                                    feature_group_count=x.shape[-1])


def chunk_gated_delta(q, k, v, g, beta, s0):
    f32 = jnp.float32
    B, H, T, DK = q.shape
    DV = v.shape[-1]
    n = T // CHUNK
    q = q.astype(f32).reshape(B, H, n, CHUNK, DK)
    k = k.astype(f32).reshape(B, H, n, CHUNK, DK)
    v = v.astype(f32).reshape(B, H, n, CHUNK, DV)
    g = g.astype(f32).reshape(B, H, n, CHUNK)
    beta = beta.astype(f32).reshape(B, H, n, CHUNK)
    gcum = jnp.cumsum(g, axis=-1)
    incl = jnp.tril(jnp.ones((CHUNK, CHUNK), dtype=bool))
    strict = jnp.tril(jnp.ones((CHUNK, CHUNK), dtype=bool), -1)
    diff = gcum[..., :, None] - gcum[..., None, :]
    decay = jnp.where(incl, jnp.exp(jnp.where(incl, diff, 0.0)), 0.0)
    k_beta = k * beta[..., None]
    v_beta = v * beta[..., None]
    a_low = jnp.where(strict, jnp.einsum('bhnid,bhnjd->bhnij', k_beta, k) * decay, 0.0)
    lmat = a_low + jnp.eye(CHUNK, dtype=f32)
    rhs = jnp.concatenate([v_beta, k_beta * jnp.exp(gcum)[..., None]], axis=-1)
    sol = lax.linalg.triangular_solve(lmat, rhs, left_side=True, lower=True, unit_diagonal=True)
    u_vals, w_keys = sol[..., :DV], sol[..., DV:]
    att = jnp.einsum('bhnid,bhnjd->bhnij', q, k) * decay
    q_dec = q * jnp.exp(gcum)[..., None]
    k_tail = k * jnp.exp(gcum[..., -1:] - gcum)[..., None]
    g_tail = jnp.exp(gcum[..., -1])
    xs = tuple(jnp.moveaxis(t, 2, 0) for t in (u_vals, w_keys, q_dec, att, k_tail, g_tail))

    def step(S, inp):
        u_n, w_n, qd_n, att_n, kt_n, gt_n = inp
        v_new = u_n - jnp.einsum('bhck,bhkv->bhcv', w_n, S)
        o_n = jnp.einsum('bhck,bhkv->bhcv', qd_n, S) + jnp.einsum('bhij,bhjv->bhiv', att_n, v_new)
        S = S * gt_n[..., None, None] + jnp.einsum('bhck,bhcv->bhkv', kt_n, v_new)
        return S, o_n

    s_fin, o = lax.scan(step, s0.astype(f32), xs)
    o = jnp.moveaxis(o, 0, 2).reshape(B, H, T, DV)
    return o, s_fin


def gdn_project(u, w_in, conv_w, a_log, dt_bias):
    B, T, _ = u.shape
    p = u @ w_in
    qkv = jax.nn.silu(short_conv(p[..., :GDN_QKV_DIM], conv_w))
    z = p[..., GDN_QKV_DIM:GDN_QKV_DIM + GDN_VALUE_DIM]
    ab = p[..., GDN_QKV_DIM + GDN_VALUE_DIM:].astype(jnp.float32).reshape(B, T, N_DIRS, 2, GDN_HEADS)
    q = qkv[..., :GDN_KEY_DIM]
    k = qkv[..., GDN_KEY_DIM:2 * GDN_KEY_DIM]
    v = qkv[..., 2 * GDN_KEY_DIM:]
    heads = lambda t: t.reshape(B, T, GDN_HEADS, GDN_HEAD_DIM).transpose(0, 2, 1, 3)
    q = l2norm(heads(q)) * (GDN_HEAD_DIM ** -0.5)
    k = l2norm(heads(k))
    v = heads(v)
    a = ab[:, :, :, 0, :].transpose(2, 0, 3, 1)
    b = ab[:, :, :, 1, :].transpose(2, 0, 3, 1)
    g = -jnp.exp(a_log.astype(jnp.float32))[:, None, :, None] * jax.nn.softplus(
        a + dt_bias.astype(jnp.float32)[:, None, :, None])
    beta = jax.nn.sigmoid(b)
    return q, k, v, z, g, beta


def gdn_out(o, z, o_gain, w_out):
    B, H, T, DV = o.shape
    o = o.transpose(0, 2, 1, 3)
    zf = z.reshape(B, T, H, DV).astype(jnp.float32)
    o = o * lax.rsqrt(jnp.mean(o * o, axis=-1, keepdims=True) + RMS_EPS) * o_gain.astype(jnp.float32) * jax.nn.silu(zf)
    return o.reshape(B, T, H * DV).astype(z.dtype) @ w_out


def gated_deltanet(u, uc, w_in, conv_w, a_log, dt_bias, o_gain, w_out, with_ctx_out):
    ql, kl, vl, zl, gl, bl = gdn_project(u, w_in, conv_w, a_log, dt_bias)
    qc, kc, vc, zc, gc, bc = gdn_project(uc, w_in, conv_w, a_log, dt_bias)
    B = u.shape[0]
    o_lat = 0.0
    o_ctx = 0.0
    for d in range(N_DIRS):
        flip = (lambda t: jnp.flip(t, axis=2)) if d == 1 else (lambda t: t)
        s0 = jnp.zeros((B, GDN_HEADS, GDN_HEAD_DIM, GDN_HEAD_DIM), jnp.float32)
        oc, s_ctx = chunk_gated_delta(flip(qc), flip(kc), flip(vc), flip(gc[d]), flip(bc[d]), s0)
        ol, _ = chunk_gated_delta(flip(ql), flip(kl), flip(vl), flip(gl[d]), flip(bl[d]), s_ctx)
        o_lat = o_lat + flip(ol)
        if with_ctx_out:
            o_ctx = o_ctx + flip(oc)
    y = gdn_out(o_lat, zl, o_gain, w_out)
    yc = gdn_out(o_ctx, zc, o_gain, w_out) if with_ctx_out else None
    return y, yc


def window_mean(x, w, axis):
    n = x.shape[axis]
    cs = jnp.cumsum(x.astype(jnp.float32), axis=axis)
    pad_width = [(0, 0)] * x.ndim
    pad_width[axis] = (1, 0)
    cs = jnp.pad(cs, pad_width)
    pos = jnp.arange(n)
    lo = jnp.clip(pos - w // 2, 0, n)
    hi = jnp.clip(pos + (w - w // 2), 0, n)
    total = jnp.take(cs, hi, axis=axis) - jnp.take(cs, lo, axis=axis)
    shape = [1] * x.ndim
    shape[axis] = n
    count = (hi - lo).astype(jnp.float32).reshape(shape)
    return (total / count).astype(x.dtype)


def pool_mixer(u, w_grp, scale, on_grid):
    B, T, D = u.shape
    if on_grid:
        rows = T // GRID_W
        xg = u.reshape(B, rows, GRID_W, POOL_GROUPS, POOL_GROUP_DIM)
        axes = (2, 1)
    else:
        xg = u.reshape(B, T, POOL_GROUPS, POOL_GROUP_DIM)
        axes = (1,)
    diffs = []
    for gi, w in enumerate(POOL_WINDOWS):
        xi = xg[..., gi, :]
        m = xi
        for ax in axes:
            m = window_mean(m, w, ax)
        diffs.append(m - xi)
    p = jnp.stack(diffs, axis=-2)
    y = jnp.einsum('...gc,gce->...ge', p, w_grp).reshape(B, T, D)
    return y * scale


def ec_moe(u, w_router, w_gate, w_up, w_down):
    B, T, D = u.shape
    cap = (EC_CAPACITY_FACTOR * T) // N_EXPERTS
    aff = jax.nn.softmax((u @ w_router).astype(jnp.float32), axis=-1)
    gates, idx = lax.top_k(jnp.swapaxes(aff, 1, 2), cap)
    bidx = jnp.arange(B)[:, None, None]
    xs = u[bidx, idx]
    hid = jax.nn.silu(jnp.einsum('becd,edf->becf', xs, w_gate)) * jnp.einsum('becd,edf->becf', xs, w_up)
    ys = jnp.einsum('becf,efd->becd', hid, w_down) * gates[..., None].astype(u.dtype)
    return jnp.zeros_like(u).at[bidx, idx].add(ys)


def setup_inputs(seed: int = 0) -> dict:
    key = jax.random.key(seed)
    ks = jax.random.split(key, 24)
    f32 = jnp.float32
    n_gdn = len(range(MIXER_GDN, DEPTH, N_MIXERS))
    n_pool = len(range(MIXER_POOL, DEPTH, N_MIXERS))
    nrm = lambda k, shape, s: jax.random.normal(k, shape, f32) * s
    dt = jnp.exp(jax.random.uniform(ks[10], (n_gdn, N_DIRS, GDN_HEADS), f32,
                                    minval=math.log(1e-3), maxval=math.log(1e-1)))
    return {
        'x': nrm(ks[0], (BATCH, SEQ, D_MODEL), 1.0),
        'c': nrm(ks[1], (BATCH, D_MODEL), 1.0),
        'ctx': nrm(ks[2], (BATCH, CTX_LEN, D_MODEL), 1.0),
        'c_ctx': nrm(ks[3], (D_MODEL,), 1.0),
        'w_mod': nrm(ks[4], (DEPTH, D_MODEL, N_MOD * D_MODEL), 0.3 * D_MODEL ** -0.5),
        'b_mod': nrm(ks[5], (DEPTH, N_MOD * D_MODEL), 0.02),
        'norm_mix_g': 1.0 + nrm(ks[6], (DEPTH, D_MODEL), 0.02),
        'norm_ffn_g': 1.0 + nrm(ks[7], (DEPTH, D_MODEL), 0.02),
        'gdn_w_in': nrm(ks[8], (n_gdn, D_MODEL, GDN_PROJ_DIM), D_MODEL ** -0.5),
        'gdn_conv_w': nrm(ks[9], (n_gdn, CONV_W, GDN_QKV_DIM), CONV_W ** -0.5),
        'gdn_a_log': jnp.log(jax.random.uniform(ks[11], (n_gdn, N_DIRS, GDN_HEADS), f32, minval=1.0, maxval=16.0)),
        'gdn_dt_bias': dt + jnp.log(-jnp.expm1(-dt)),
        'gdn_o_gain': 1.0 + nrm(ks[12], (n_gdn, GDN_HEAD_DIM), 0.02),
        'gdn_w_out': nrm(ks[13], (n_gdn, GDN_VALUE_DIM, D_MODEL), GDN_VALUE_DIM ** -0.5),
        'pool_w': nrm(ks[14], (n_pool, POOL_GROUPS, POOL_GROUP_DIM, POOL_GROUP_DIM), POOL_GROUP_DIM ** -0.5),
        'pool_scale': 1.0 + nrm(ks[15], (n_pool, D_MODEL), 0.05),
        'moe_w_router': nrm(ks[16], (DEPTH, D_MODEL, N_EXPERTS), D_MODEL ** -0.5),
        'moe_w_gate': nrm(ks[17], (DEPTH, N_EXPERTS, D_MODEL, D_EXPERT), D_MODEL ** -0.5),
        'moe_w_up': nrm(ks[18], (DEPTH, N_EXPERTS, D_MODEL, D_EXPERT), D_MODEL ** -0.5),
        'moe_w_down': nrm(ks[19], (DEPTH, N_EXPERTS, D_EXPERT, D_MODEL), D_EXPERT ** -0.5),
        'final_g': 1.0 + nrm(ks[20], (D_MODEL,), 0.02),
    }


def reference(x, c, ctx, c_ctx, w_mod, b_mod, norm_mix_g, norm_ffn_g,
              gdn_w_in, gdn_conv_w, gdn_a_log, gdn_dt_bias, gdn_o_gain, gdn_w_out,
              pool_w, pool_scale, moe_w_router, moe_w_gate, moe_w_up, moe_w_down, final_g):
    h, hc = x, ctx
    c_act = jax.nn.silu(c)
    cc_act = jax.nn.silu(c_ctx)
    for i in range(DEPTH):
        kind = i % N_MIXERS
        j = i // N_MIXERS
        ctx_carry = any(l % N_MIXERS == MIXER_GDN for l in range(i + 1, DEPTH))
        mod = c_act @ w_mod[i] + b_mod[i]
        sh_m, sc_m, gt_m, sh_f, sc_f, gt_f = [t[:, None, :] for t in jnp.split(mod, N_MOD, axis=-1)]
        u = modulate(rmsnorm(h, norm_mix_g[i]), sh_m, sc_m)
        uc = None
        mod_c = None
        if kind == MIXER_GDN or ctx_carry:
            mod_c = jnp.split(cc_act @ w_mod[i] + b_mod[i], N_MOD, axis=-1)
            uc = modulate(rmsnorm(hc, norm_mix_g[i]), mod_c[0], mod_c[1])
        if kind == MIXER_GDN:
            y, yc = gated_deltanet(u, uc, gdn_w_in[j], gdn_conv_w[j], gdn_a_log[j], gdn_dt_bias[j],
                                   gdn_o_gain[j], gdn_w_out[j], ctx_carry)
        else:
            y = pool_mixer(u, pool_w[j], pool_scale[j], True)
            yc = pool_mixer(uc, pool_w[j], pool_scale[j], False) if ctx_carry else None
        h = h + gt_m * y
        h = h + gt_f * ec_moe(modulate(rmsnorm(h, norm_ffn_g[i]), sh_f, sc_f),
                              moe_w_router[i], moe_w_gate[i], moe_w_up[i], moe_w_down[i])
        if ctx_carry:
            hc = hc + mod_c[2] * yc
            hc = hc + mod_c[5] * ec_moe(modulate(rmsnorm(hc, norm_ffn_g[i]), mod_c[3], mod_c[4]),
                                        moe_w_router[i], moe_w_gate[i], moe_w_up[i], moe_w_down[i])
    return rmsnorm(h, final_g)
```

```python
import functools

import jax
import jax.numpy as jnp
from jax import lax
from jax.experimental import pallas as pl
from jax.experimental.pallas import tpu as pltpu

F32, BF16, I32 = jnp.float32, jnp.bfloat16, jnp.int32

GRID_W = 64
N_DIRS = 2
HEADS = 8
HEAD_DIM = 128
CONV_W = 5
POOL_WINDOWS = (2, 4, 8, 16)
N_EXPERTS = 16
EC_CAPACITY_FACTOR = 2
N_MOD = 6
RMS_EPS = 1e-6

LANES = 128
SUBLANES = 8
VMEM_LIMIT_BYTES = 58 * 1024 * 1024

CHUNK = 128
MOD_ROWS = 8
GATE_LANES = 16
GDN_PIECES = {"scores": (1, 1), "inverse": (2, 2), "solve": (1, 1), "state": (1, 1)}


def _sigmoid(x):
    return 1.0 / (1.0 + jnp.exp(-x))


def _silu(x):
    return x * _sigmoid(x)


def _softplus(x):
    return jnp.maximum(x, 0.0) + jnp.log(1.0 + jnp.exp(-jnp.abs(x)))


def _pieces(x, n):
    out = []
    r = x
    for i in range(n):
        p = r.astype(BF16)
        out.append(p)
        if i + 1 < n:
            r = r - p.astype(F32)
    return out


_NN = (((1,), (0,)), ((), ()))
_NT = (((1,), (1,)), ((), ()))
_TN = (((0,), (0,)), ((), ()))


def _dg(a, b, dims):
    return lax.dot_general(a, b, dims, preferred_element_type=F32)


def _mm(a, b, dims=_NN):
    return _dg(a.astype(BF16), b.astype(BF16), dims)


def _mmp(a, b, na, nb, dims=_NN):
    pa = _pieces(a, na) if na > 1 else [a.astype(BF16)]
    pb = _pieces(b, nb) if nb > 1 else [b.astype(BF16)]
    order = max(na, nb)
    acc = None
    for i, x in enumerate(pa):
        for j, y in enumerate(pb):
            if i + j < order:
                t = _dg(x, y, dims)
                acc = t if acc is None else acc + t
    return acc


def _norm_mod(x, gain, shift, scale):
    ms = jnp.mean(x * x, axis=-1, keepdims=True)
    y = x * lax.rsqrt(ms + RMS_EPS) * gain
    return y * (1.0 + scale) + shift


def _cparams(sem):
    return pltpu.CompilerParams(dimension_semantics=sem, vmem_limit_bytes=VMEM_LIMIT_BYTES)


def _mod_kernel(cc_ref, w_ref, b_ref, o_ref):
    act = _silu(cc_ref[...])
    o_ref[0] = _mmp(act, w_ref[0], 2, 2) + b_ref[0]


def _modulation(cc, w_mod, b_mod):
    depth, d, nd = w_mod.shape
    return pl.pallas_call(
        _mod_kernel,
        out_shape=jax.ShapeDtypeStruct((depth, MOD_ROWS, nd), F32),
        grid=(depth, nd // d),
        in_specs=[
            pl.BlockSpec((MOD_ROWS, d), lambda l, j: (0, 0)),
            pl.BlockSpec((1, d, d), lambda l, j: (l, 0, j)),
            pl.BlockSpec((1, 1, d), lambda l, j: (l, 0, j)),
        ],
        out_specs=pl.BlockSpec((1, MOD_ROWS, d), lambda l, j: (l, 0, j)),
        compiler_params=_cparams(("parallel", "parallel")),
        name="modulation",
    )(cc, w_mod, b_mod.reshape(depth, 1, nd))


def _proj_kernel(x_ref, g_ref, sh_ref, sc_ref, w_ref, wab_ref, alog_ref, dtb_ref,
                 p_ref, cp_ref, u_s, *, tm):
    @pl.when(pl.program_id(2) == 0)
    def _():
        gain = g_ref[...]
        shift = sh_ref[0]
        scale = sc_ref[0]
        lane = lax.broadcasted_iota(I32, (CHUNK, LANES), 1)
        row = lax.broadcasted_iota(I32, (CHUNK, CHUNK), 0)
        col = lax.broadcasted_iota(I32, (CHUNK, CHUNK), 1)
        lower = jnp.where(col <= row, 1.0, 0.0).astype(BF16)
        upper = jnp.where(col >= row, 1.0, 0.0).astype(BF16)
        ones = jnp.ones((CHUNK, CHUNK), BF16)
        is_decay = (lane % GATE_LANES < HEADS) & (lane < N_DIRS * GATE_LANES)

        def body(r, carry):
            rows = pl.ds(pl.multiple_of(r * CHUNK, CHUNK), CHUNK)
            u = _norm_mod(x_ref[0, rows, :], gain, shift, scale)
            u_s[rows, :] = u.astype(BF16)
            ab = _mmp(u, wab_ref[...], 2, 2)
            g = -jnp.exp(alog_ref[...]) * _softplus(ab + dtb_ref[...])
            g = jnp.where(is_decay, g, 0.0)
            beta = _sigmoid(ab)
            gp = _pieces(g, 3)
            gpre = sum(_dg(lower, x, _NN) for x in gp)
            gsuf = sum(_dg(upper, x, _NN) for x in gp)
            gtot = sum(_dg(ones, x, _NN) for x in gp)
            gc = jnp.where(lane < GATE_LANES, gpre, gsuf)
            pk = jnp.where(is_decay, gc, beta)
            pk = jnp.where(lane < N_DIRS * GATE_LANES, pk,
                           pltpu.roll(gtot, N_DIRS * GATE_LANES, axis=1))
            cp_ref[0, rows, :] = jnp.where(lane < 2 * N_DIRS * GATE_LANES, pk, 0.0)
            return carry

        lax.fori_loop(0, tm // CHUNK, body, 0)

    p_ref[0] = jnp.dot(u_s[...], w_ref[...], preferred_element_type=F32)


def _gdn_project(x, gain, shift, scale, w_main, w_ab, alog_l, dtb_l, tm):
    b, t, d = x.shape
    n_main = w_main.shape[1]
    tn = 1024
    return pl.pallas_call(
        functools.partial(_proj_kernel, tm=tm),
        out_shape=(jax.ShapeDtypeStruct((b, t, n_main), F32),
                   jax.ShapeDtypeStruct((b, t, LANES), F32)),
        grid=(b, t // tm, n_main // tn),
        in_specs=[
            pl.BlockSpec((1, tm, d), lambda i, m, j: (i, m, 0)),
            pl.BlockSpec((1, d), lambda i, m, j: (0, 0)),
            pl.BlockSpec((1, 1, d), lambda i, m, j: (i, 0, 0)),
            pl.BlockSpec((1, 1, d), lambda i, m, j: (i, 0, 0)),
            pl.BlockSpec((d, tn), lambda i, m, j: (0, j)),
            pl.BlockSpec((d, LANES), lambda i, m, j: (0, 0)),
            pl.BlockSpec((1, LANES), lambda i, m, j: (0, 0)),
            pl.BlockSpec((1, LANES), lambda i, m, j: (0, 0)),
        ],
        out_specs=(pl.BlockSpec((1, tm, tn), lambda i, m, j: (i, m, j)),
                   pl.BlockSpec((1, tm, LANES), lambda i, m, j: (i, m, 0))),
        scratch_shapes=[pltpu.VMEM((tm, d), BF16)],
        compiler_params=_cparams(("parallel", "parallel", "arbitrary")),
        name="gdn_project",
    )(x, gain.reshape(1, d), shift, scale, w_main, w_ab, alog_l, dtb_l)


def _neumann_inverse(a, eye):
    na, nb = GDN_PIECES["inverse"]
    p = eye - a
    b = _mmp(a, a, na, nb)
    levels = CHUNK.bit_length() - 1
    for lvl in range(1, levels):
        if lvl + 1 < levels:
            pb = _mmp(jnp.concatenate([p, b], axis=0), b, na, nb)
            p = p + pb[:CHUNK]
            b = pb[CHUNK:]
        else:
            p = p + _mmp(p, b, na, nb)
    return p


def _gdn_kernel(qp_ref, kp_ref, vp_ref, zp_ref, cp_ref, rp_ref, cwq_ref, cwk_ref, cwv_ref,
                gain_ref, s0_ref, *rest, t, with_out):
    if with_out:
        og_ref, sfin_ref = rest[:2]
        pad_s, q_s, k_s, v_s, cph_s, of_s, ob_s, st_s = rest[2:]
    else:
        (sfin_ref,) = rest[:1]
        pad_s, q_s, k_s, v_s, cph_s, of_s, ob_s, st_s = rest[1:]
    head = pl.program_id(1)
    n_chunks = t // CHUNK
    halo = SUBLANES

    row = lax.broadcasted_iota(I32, (CHUNK, CHUNK), 0)
    col = lax.broadcasted_iota(I32, (CHUNK, CHUNK), 1)

    kind = col // N_DIRS
    direction = col % N_DIRS
    src = jnp.where(kind == 0, direction * GATE_LANES + head,
                    jnp.where(kind == 1, direction * GATE_LANES + HEADS + head,
                              N_DIRS * GATE_LANES + direction * GATE_LANES + head))
    perm = jnp.where((row == src) & (col < 3 * N_DIRS), 1.0, 0.0).astype(BF16)

    def pick(r, carry):
        rows = pl.ds(pl.multiple_of(r * CHUNK, CHUNK), CHUNK)
        cph_s[rows, :] = sum(_dg(x, perm, _NN) for x in _pieces(cp_ref[0, rows, :], 3))
        return carry

    lax.fori_loop(0, n_chunks, pick, 0)

    tile = 256
    zeros_halo = jnp.zeros((halo, HEAD_DIM), F32)
    for src_ref, cw_ref, dst, mode in ((qp_ref, cwq_ref, q_s, "q"), (kp_ref, cwk_ref, k_s, "k"),
                                       (vp_ref, cwv_ref, v_s, "v")):
        pad_s[0:halo, :] = zeros_halo
        pad_s[t + halo:t + 2 * halo, :] = zeros_halo

        def fill(r, carry, src_ref=src_ref):
            rows = pl.multiple_of(r * tile, tile)
            pad_s[pl.ds(rows + halo, tile), :] = src_ref[0, pl.ds(rows, tile), :]
            return carry

        lax.fori_loop(0, t // tile, fill, 0)
        cw = cw_ref[...]

        def conv(r, carry, cw=cw, dst=dst, mode=mode):
            rows = pl.multiple_of(r * tile, tile)
            win = pad_s[pl.ds(rows, tile + 2 * halo), :]
            acc = None
            for j in range(CONV_W):
                off = halo + j - CONV_W // 2
                term = cw[j:j + 1, :] * win[off:off + tile, :]
                acc = term if acc is None else acc + term
            y = _silu(acc)
            if mode != "v":
                y = y * lax.rsqrt(jnp.sum(y * y, axis=-1, keepdims=True) + RMS_EPS)
            if mode == "q":
                y = y * (HEAD_DIM ** -0.5)
            dst[pl.ds(rows, tile), :] = y
            return carry

        lax.fori_loop(0, t // tile, conv, 0)

    st_s[...] = s0_ref[0, 0]
    eye = jnp.where(row == col, 1.0, 0.0)
    masks = ((col <= row, col < row), (col >= row, col > row))

    def step(d, c):
        rows = pl.ds(pl.multiple_of(c * CHUNK, CHUNK), CHUNK)
        q = q_s[rows, :]
        k = k_s[rows, :]
        v = v_s[rows, :]
        cp = cph_s[rows, :]
        gc = jnp.broadcast_to(cp[:, d:d + 1], (CHUNK, HEAD_DIM))
        beta = jnp.broadcast_to(cp[:, N_DIRS + d:N_DIRS + d + 1], (CHUNK, HEAD_DIM))
        gtot = jnp.broadcast_to(cp[:, 2 * N_DIRS + d:2 * N_DIRS + d + 1], (CHUNK, HEAD_DIM))
        grow = jnp.broadcast_to(rp_ref[0, 0, c][d:d + 1, :], (CHUNK, CHUNK))
        incl, strict = masks[d]
        decay = jnp.where(incl, jnp.exp(jnp.where(incl, gc - grow, 0.0)), 0.0)
        e_in = jnp.exp(gc)
        e_out = jnp.exp(gtot - gc)
        kb = k * beta
        kq = _mmp(jnp.concatenate([kb, q], axis=0), k, *GDN_PIECES["scores"], _NT)
        a = jnp.where(strict, kq[:CHUNK] * decay, 0.0)
        att = kq[CHUNK:] * decay
        tinv = _neumann_inverse(a, eye)
        uw = _mmp(tinv, jnp.concatenate([v * beta, kb * e_in], axis=1), *GDN_PIECES["solve"])
        state = st_s[d]
        wq = _mmp(jnp.concatenate([uw[:, HEAD_DIM:], q * e_in], axis=0), state, *GDN_PIECES["state"])
        v_new = uw[:, :HEAD_DIM] - wq[:CHUNK]
        o = wq[CHUNK:] + _mmp(att, v_new, *GDN_PIECES["state"])
        st_s[d] = state * jnp.exp(gtot) + _mmp(k * e_out, v_new, *GDN_PIECES["state"], _TN)
        if d == 0:
            of_s[rows, :] = o
        else:
            ob_s[rows, :] = o

    def body(c, carry):
        step(0, c)
        step(1, n_chunks - 1 - c)
        return carry

    lax.fori_loop(0, n_chunks, body, 0)
    sfin_ref[0, 0] = st_s[...]

    if with_out:
        gain = gain_ref[...]

        def emit(r, carry):
            rows = pl.ds(pl.multiple_of(r * tile, tile), tile)
            o = of_s[rows, :] + ob_s[rows, :]
            o = o * lax.rsqrt(jnp.mean(o * o, axis=-1, keepdims=True) + RMS_EPS) * gain
            og_ref[0, rows, :] = (o * _silu(zp_ref[0, rows, :])).astype(BF16)
            return carry

        lax.fori_loop(0, t // tile, emit, 0)


def _gdn_scan(p, cp, rp, conv_w, o_gain, s0, with_out):
    b, t, _ = p.shape
    hd = HEAD_DIM
    col_spec = lambda off: pl.BlockSpec((1, t, hd), lambda i, h, off=off: (i, 0, off + h))
    cw_spec = lambda off: pl.BlockSpec((CONV_W, hd), lambda i, h, off=off: (0, off + h))
    state_spec = pl.BlockSpec((1, 1, N_DIRS, hd, hd), lambda i, h: (i, h, 0, 0, 0))
    out_shape = [jax.ShapeDtypeStruct((b, HEADS, N_DIRS, hd, hd), F32)]
    out_specs = [state_spec]
    if with_out:
        out_shape = [jax.ShapeDtypeStruct((b, t, HEADS * hd), BF16)] + out_shape
        out_specs = [pl.BlockSpec((1, t, hd), lambda i, h: (i, 0, h))] + out_specs
    halo = SUBLANES
    res = pl.pallas_call(
        functools.partial(_gdn_kernel, t=t, with_out=with_out),
        out_shape=tuple(out_shape),
        grid=(b, HEADS),
        in_specs=[
            col_spec(0), col_spec(HEADS), col_spec(2 * HEADS), col_spec(3 * HEADS),
            pl.BlockSpec((1, t, LANES), lambda i, h: (i, 0, 0)),
            pl.BlockSpec((1, 1, t // CHUNK, SUBLANES, CHUNK), lambda i, h: (i, h, 0, 0, 0)),
            cw_spec(0), cw_spec(HEADS), cw_spec(2 * HEADS),
            pl.BlockSpec((1, hd), lambda i, h: (0, 0)),
            state_spec,
        ],
        out_specs=tuple(out_specs),
        scratch_shapes=[
            pltpu.VMEM((t + 2 * halo, hd), F32),
            pltpu.VMEM((t, hd), F32), pltpu.VMEM((t, hd), F32), pltpu.VMEM((t, hd), F32),
            pltpu.VMEM((t, LANES), F32),
            pltpu.VMEM((t, hd), F32), pltpu.VMEM((t, hd), F32),
            pltpu.VMEM((N_DIRS, hd, hd), F32),
        ],
        compiler_params=_cparams(("parallel", "arbitrary")),
        name="gdn_scan_out" if with_out else "gdn_scan_ctx",
    )(p, p, p, p, cp, rp, conv_w, conv_w, conv_w, o_gain.reshape(1, hd), s0)
    return res


def _row_gates(cp):
    b, t, _ = cp.shape
    g = cp[:, :, :N_DIRS * GATE_LANES].reshape(b, t // CHUNK, CHUNK, N_DIRS, GATE_LANES)[..., :HEADS]
    g = g.transpose(0, 4, 1, 3, 2)
    return jnp.pad(g, ((0, 0), (0, 0), (0, 0), (0, SUBLANES - N_DIRS), (0, 0)))


def _ffn_prep_kernel(*refs, tm, with_proj):
    if with_proj:
        (og_ref, wo_ref, x_ref, gt_ref, g_ref, sh_ref, sc_ref, wrt_ref, wrc_ref,
         h_ref, uf_ref, lgt_ref, lgc_ref) = refs
    else:
        (x_ref, g_ref, sh_ref, sc_ref, wrt_ref, wrc_ref, uf_ref, lgt_ref, lgc_ref) = refs
    gain = g_ref[...]
    shift = sh_ref[0]
    scale = sc_ref[0]
    sub = 256

    for r in range(tm // sub):
        rows = slice(r * sub, (r + 1) * sub)
        h = x_ref[0, rows, :]
        if with_proj:
            h = h + gt_ref[0] * jnp.dot(og_ref[0, rows, :], wo_ref[...], preferred_element_type=F32)
            h_ref[0, rows, :] = h
        u = _norm_mod(h, gain, shift, scale)
        uf_ref[0, rows, :] = u
        lgc_ref[0, rows, :] = _mmp(u, wrc_ref[...], 2, 2)
        lgt_ref[0, :, rows] = _mmp(wrt_ref[...], u, 2, 2, _NT)


def _ffn_prep(x, gain, shift, scale, w_router, proj=None):
    b, t, d = x.shape
    e = w_router.shape[1]
    tm = 512
    wrt = w_router.T
    wrc = jnp.pad(w_router, ((0, 0), (0, LANES - e)))
    row_spec = pl.BlockSpec((1, tm, d), lambda i, m: (i, m, 0))
    vec_spec = pl.BlockSpec((1, 1, d), lambda i, m: (i, 0, 0))
    common_in = [pl.BlockSpec((1, d), lambda i, m: (0, 0)), vec_spec, vec_spec,
                 pl.BlockSpec((e, d), lambda i, m: (0, 0)), pl.BlockSpec((d, LANES), lambda i, m: (0, 0))]
    common_args = [gain.reshape(1, d), shift, scale, wrt, wrc]
    outs = [jax.ShapeDtypeStruct((b, t, d), F32), jax.ShapeDtypeStruct((b, e, t), F32),
            jax.ShapeDtypeStruct((b, t, LANES), F32)]
    out_specs = [row_spec, pl.BlockSpec((1, e, tm), lambda i, m: (i, 0, m)),
                 pl.BlockSpec((1, tm, LANES), lambda i, m: (i, m, 0))]
    if proj is not None:
        og, w_out, gate = proj
        in_specs = [row_spec, pl.BlockSpec((d, d), lambda i, m: (0, 0)), row_spec, vec_spec] + common_in
        args = [og, w_out, x, gate] + common_args
        outs = [jax.ShapeDtypeStruct((b, t, d), F32)] + outs
        out_specs = [row_spec] + out_specs
    else:
        in_specs = [row_spec] + common_in
        args = [x] + common_args
    return pl.pallas_call(
        functools.partial(_ffn_prep_kernel, tm=tm, with_proj=proj is not None),
        out_shape=tuple(outs), grid=(b, t // tm), in_specs=in_specs, out_specs=tuple(out_specs),
        compiler_params=_cparams(("parallel", "parallel")),
        name="ffn_prep_proj" if proj is not None else "ffn_prep",
    )(*args)


def _residual_kernel(h_ref, m_ref, gt_ref, g_ref, *refs, tm, mode):
    sub = 256
    gain = g_ref[...]

    def body(r, carry):
        rows = pl.ds(pl.multiple_of(r * sub, sub), sub)
        h = h_ref[0, rows, :] + gt_ref[0] * m_ref[0, rows, :]
        if mode == "mix":
            sh_ref, sc_ref, ho_ref, u_ref = refs
            ho_ref[0, rows, :] = h
            u_ref[0, rows, :] = _norm_mod(h, gain, sh_ref[0], sc_ref[0])
        else:
            (o_ref,) = refs
            ms = jnp.mean(h * h, axis=-1, keepdims=True)
            o_ref[0, rows, :] = h * lax.rsqrt(ms + RMS_EPS) * gain
        return carry

    lax.fori_loop(0, tm // sub, body, 0)


def _residual(h, m, gate, gain, shift=None, scale=None):
    b, t, d = h.shape
    tm = 512
    row_spec = pl.BlockSpec((1, tm, d), lambda i, j: (i, j, 0))
    vec_spec = pl.BlockSpec((1, 1, d), lambda i, j: (i, 0, 0))
    in_specs = [row_spec, row_spec, vec_spec, pl.BlockSpec((1, d), lambda i, j: (0, 0))]
    args = [h, m, gate, gain.reshape(1, d)]
    if shift is not None:
        mode = "mix"
        in_specs += [vec_spec, vec_spec]
        args += [shift, scale]
        outs = (jax.ShapeDtypeStruct((b, t, d), F32), jax.ShapeDtypeStruct((b, t, d), F32))
        out_specs = (row_spec, row_spec)
    else:
        mode = "final"
        outs = jax.ShapeDtypeStruct((b, t, d), F32)
        out_specs = row_spec
    return pl.pallas_call(
        functools.partial(_residual_kernel, tm=tm, mode=mode),
        out_shape=outs, grid=(b, t // tm), in_specs=in_specs, out_specs=out_specs,
        compiler_params=_cparams(("parallel", "parallel")),
        name="residual_" + mode,
    )(*args)


def _pool_kernel(u_ref, h_ref, gt_ref, w_ref, sc_ref, o_ref, m1_s, *, t):
    group = pl.program_id(1)
    tile = 256
    rows_per_tile = tile // GRID_W
    n_rows = t // GRID_W
    halo = max(POOL_WINDOWS) // 2 * GRID_W
    ti = lax.broadcasted_iota(I32, (tile, tile), 0)
    si = lax.broadcasted_iota(I32, (tile, tile), 1)
    tcol = lax.broadcasted_iota(I32, (tile, 1), 0)

    def run(win):
        lo = win // 2
        hi = win - lo
        off = si - ti
        band = jnp.where((ti // GRID_W == si // GRID_W) & (off >= -lo) & (off < hi), 1.0, 0.0).astype(BF16)
        cpos = tcol % GRID_W
        inv_c = 1.0 / (jnp.minimum(cpos + hi, GRID_W) - jnp.maximum(cpos - lo, 0)).astype(F32)
        m1_s[0:halo, :] = jnp.zeros((halo, m1_s.shape[1]), F32)
        m1_s[halo + t:2 * halo + t, :] = jnp.zeros((halo, m1_s.shape[1]), F32)

        def along_w(r, carry):
            rows = pl.multiple_of(r * tile, tile)
            x = u_ref[0, pl.ds(rows, tile), :]
            tot = sum(_dg(band, p, _NN) for p in _pieces(x, 3))
            m1_s[pl.ds(rows + halo, tile), :] = tot * inv_c
            return carry

        lax.fori_loop(0, t // tile, along_w, 0)
        w = w_ref[0]
        scale = sc_ref[...]
        gate = gt_ref[0]

        def along_h(r, carry):
            rows = pl.multiple_of(r * tile, tile)
            acc = None
            for o in range(-lo, hi):
                term = m1_s[pl.ds(rows + halo + o * GRID_W, tile), :]
                acc = term if acc is None else acc + term
            rpos = r * rows_per_tile + tcol // GRID_W
            inv_r = 1.0 / (jnp.minimum(rpos + hi, n_rows) - jnp.maximum(rpos - lo, 0)).astype(F32)
            x = u_ref[0, pl.ds(rows, tile), :]
            y = _mmp(acc * inv_r - x, w, 2, 2) * scale
            o_ref[0, pl.ds(rows, tile), :] = h_ref[0, pl.ds(rows, tile), :] + gate * y
            return carry

        lax.fori_loop(0, t // tile, along_h, 0)

    for gi, win in enumerate(POOL_WINDOWS):
        pl.when(group == gi)(functools.partial(run, win))


def _pool_mixer(u, h, gate, w_grp, scale):
    b, t, d = u.shape
    groups = len(POOL_WINDOWS)
    gd = d // groups
    halo = max(POOL_WINDOWS) // 2 * GRID_W
    blk = pl.BlockSpec((1, t, gd), lambda i, g: (i, 0, g))
    return pl.pallas_call(
        functools.partial(_pool_kernel, t=t),
        out_shape=jax.ShapeDtypeStruct((b, t, d), F32),
        grid=(b, groups),
        in_specs=[blk, blk, pl.BlockSpec((1, 1, gd), lambda i, g: (i, 0, g)),
                  pl.BlockSpec((1, gd, gd), lambda i, g: (g, 0, 0)),
                  pl.BlockSpec((1, gd), lambda i, g: (0, g))],
        out_specs=blk,
        scratch_shapes=[pltpu.VMEM((t + 2 * halo, gd), F32)],
        compiler_params=_cparams(("parallel", "parallel")),
        name="pool_mixer",
    )(u, h, gate, w_grp, scale.reshape(1, d))


def _exclusive_count(mask, tri, ones):
    n = mask.shape[1] // LANES
    run = jnp.zeros((mask.shape[0], LANES), F32)
    maskf = jnp.where(mask, 1.0, 0.0)
    tiles = []
    for j in range(n):
        m = maskf[:, j * LANES:(j + 1) * LANES].astype(BF16)
        tiles.append(_dg(m, tri, _NN) + run)
        run = run + _dg(m, ones, _NN)
    return jnp.concatenate(tiles, axis=1), run


def _route_kernel(lgt_ref, lgc_ref, o_ref, slot_s, rhs_s, *, t, cap, tk):
    e = lgt_ref.shape[1]
    lg = lgt_ref[0]
    ex = jnp.exp(lg - jnp.max(lg, axis=0, keepdims=True))
    aff = ex / jnp.sum(ex, axis=0, keepdims=True)
    key = pltpu.bitcast(aff, I32)

    def search(i, cur):
        cand = cur | jnp.left_shift(jnp.int32(1), 30 - i)
        cnt = jnp.sum(jnp.where(key >= cand, 1.0, 0.0), axis=1, keepdims=True)
        return jnp.where(cnt >= cap, cand, cur)

    thr = lax.fori_loop(0, 31, search, jnp.zeros((e, 1), I32))
    above = key > thr
    equal = key == thr
    ri = lax.broadcasted_iota(I32, (LANES, LANES), 0)
    ci = lax.broadcasted_iota(I32, (LANES, LANES), 1)
    tri = jnp.where(ri < ci, 1.0, 0.0).astype(BF16)
    ones = jnp.ones((LANES, LANES), BF16)
    n_above = jnp.sum(jnp.where(above, 1.0, 0.0), axis=1, keepdims=True)
    eq_rank, _ = _exclusive_count(equal, tri, ones)
    sel = above | (equal & (eq_rank < cap - n_above))
    slot, _ = _exclusive_count(sel, tri, ones)
    slot = jnp.where(sel, slot, -1.0)
    per = tk // LANES
    for j in range(t // LANES):
        for x in range(e):
            slot_s[j // per, x, :, (j % per) * LANES:(j % per + 1) * LANES] = slot[x:x + 1, j * LANES:(j + 1) * LANES]

    lane = lax.broadcasted_iota(I32, (tk, LANES), 1)
    tok0 = lax.broadcasted_iota(I32, (tk, LANES), 0)

    def build(r, carry):
        rows = pl.ds(pl.multiple_of(r * tk, tk), tk)
        lc = jnp.where(lane < e, lgc_ref[0, rows, :], -jnp.inf)
        exc = jnp.exp(lc - jnp.max(lc, axis=1, keepdims=True))
        affc = exc / jnp.sum(exc, axis=1, keepdims=True)
        p0, p1, p2 = [p.astype(F32) for p in _pieces(affc, 3)]
        packed = (pltpu.roll(p0, 2, axis=1) + pltpu.roll(p1, 2 + e, axis=1)
                  + pltpu.roll(p2, 2 + 2 * e, axis=1))
        tok = tok0 + r * tk
        packed = jnp.where(lane == 0, (tok // GRID_W).astype(F32),
                           jnp.where(lane == 1, (tok % GRID_W).astype(F32), packed))
        rhs_s[rows, :] = packed.astype(BF16)
        return carry

    lax.fori_loop(0, t // tk, build, 0)

    s_iota = lax.broadcasted_iota(I32, (cap, tk), 0).astype(F32)

    def compact(i, carry):
        x = i // (t // tk)
        j = i % (t // tk)
        onehot = jnp.where(s_iota == slot_s[j, x], 1.0, 0.0).astype(BF16)
        part = _dg(onehot, rhs_s[pl.ds(pl.multiple_of(j * tk, tk), tk), :], _NN)

        @pl.when(j == 0)
        def _():
            o_ref[0, x] = part

        @pl.when(j != 0)
        def _():
            o_ref[0, x] = o_ref[0, x] + part

        return carry

    lax.fori_loop(0, e * (t // tk), compact, 0)


def _route(lgt, lgc, cap):
    b, e, t = lgt.shape
    tk = 512
    packed = pl.pallas_call(
        functools.partial(_route_kernel, t=t, cap=cap, tk=tk),
        out_shape=jax.ShapeDtypeStruct((b, e, cap, LANES), F32),
        grid=(b,),
        in_specs=[pl.BlockSpec((1, e, t), lambda i: (i, 0, 0)),
                  pl.BlockSpec((1, t, LANES), lambda i: (i, 0, 0))],
        out_specs=pl.BlockSpec((1, e, cap, LANES), lambda i: (i, 0, 0, 0)),
        scratch_shapes=[pltpu.VMEM((t // tk, e, 1, tk), F32), pltpu.VMEM((t, LANES), BF16)],
        compiler_params=_cparams(("parallel",)),
        name="route_topk",
    )(lgt, lgc)
    idx = (packed[..., 0] * GRID_W + packed[..., 1]).astype(I32)
    pieces = packed[..., 2:2 + 3 * e].reshape(b, e, cap, 3, e).sum(axis=3)
    gates = jnp.take_along_axis(pieces, jnp.arange(e).reshape(1, e, 1, 1), axis=3)[..., 0]
    return idx, gates


def _ffn_kernel(idx_ref, uf_hbm, wg_ref, wu_ref, wd_ref, ys_ref, xb_s, wg_s, wu_s, wd_s, sem, *, rows, tm):
    f = pl.program_id(1)

    @pl.when(f == 0)
    def _():
        def gather(x_s):
            def issue(r, carry):
                pltpu.make_async_copy(uf_hbm.at[pl.ds(idx_ref[0, 0, r], 1), :],
                                      x_s.at[pl.ds(r, 1), :], sem).start()
                return carry

            lax.fori_loop(0, rows, issue, 0)

            def drain(r, carry):
                pltpu.make_async_copy(uf_hbm.at[pl.ds(0, 1), :], x_s.at[pl.ds(r, 1), :], sem).wait()
                return carry

            lax.fori_loop(0, rows, drain, 0)

            def cast(r, carry):
                rr = pl.ds(pl.multiple_of(r * tm, tm), tm)
                xb_s[rr, :] = x_s[rr, :].astype(BF16)
                return carry

            lax.fori_loop(0, rows // tm, cast, 0)

        pl.run_scoped(gather, pltpu.VMEM((rows, uf_hbm.shape[1]), F32))

    wg_s[...] = wg_ref[0].astype(BF16)
    wu_s[...] = wu_ref[0].astype(BF16)
    wd_s[...] = wd_ref[0].astype(BF16)

    def body(m, carry):
        rr = pl.ds(pl.multiple_of(m * tm, tm), tm)
        x = xb_s[rr, :]
        g = jnp.dot(x, wg_s[...], preferred_element_type=F32)
        u = jnp.dot(x, wu_s[...], preferred_element_type=F32)
        hid = (_silu(g) * u).astype(BF16)
        y = jnp.dot(hid, wd_s[...], preferred_element_type=F32)

        @pl.when(f == 0)
        def _():
            ys_ref[0, rr, :] = y

        @pl.when(f != 0)
        def _():
            ys_ref[0, rr, :] = ys_ref[0, rr, :] + y

        return carry

    lax.fori_loop(0, rows // tm, body, 0)


def _expert_ffn(uf_flat, gidx, w_gate, w_up, w_down):
    e, d, de = w_gate.shape
    rows = gidx.shape[-1]
    tf = 512
    tm = 512
    return pl.pallas_call(
        functools.partial(_ffn_kernel, rows=rows, tm=tm),
        out_shape=jax.ShapeDtypeStruct((e, rows, d), F32),
        grid=(e, de // tf),
        in_specs=[
            pl.BlockSpec((1, 1, rows), lambda x, f: (x, 0, 0), memory_space=pltpu.SMEM),
            pl.BlockSpec(memory_space=pl.ANY),
            pl.BlockSpec((1, d, tf), lambda x, f: (x, 0, f)),
            pl.BlockSpec((1, d, tf), lambda x, f: (x, 0, f)),
            pl.BlockSpec((1, tf, d), lambda x, f: (x, f, 0)),
        ],
        out_specs=pl.BlockSpec((1, rows, d), lambda x, f: (x, 0, 0)),
        scratch_shapes=[pltpu.VMEM((rows, d), BF16), pltpu.VMEM((d, tf), BF16), pltpu.VMEM((d, tf), BF16),
                        pltpu.VMEM((tf, d), BF16), pltpu.SemaphoreType.DMA],
        compiler_params=_cparams(("arbitrary", "arbitrary")),
        name="expert_ffn",
    )(gidx, uf_flat, w_gate, w_up, w_down)


def _combine_kernel(idx_ref, gate_ref, ys_ref, o_ref, *, cap, t):
    @pl.when(pl.program_id(1) == 0)
    def _():
        tile = 256

        def clear(r, carry):
            o_ref[0, pl.ds(pl.multiple_of(r * tile, tile), tile), :] = jnp.zeros((tile, o_ref.shape[2]), F32)
            return carry

        lax.fori_loop(0, t // tile, clear, 0)

    def body(s, carry):
        tok = idx_ref[0, 0, s]
        o_ref[0, pl.ds(tok, 1), :] = o_ref[0, pl.ds(tok, 1), :] + gate_ref[0, 0, s] * ys_ref[0, pl.ds(s, 1), :]
        return carry

    lax.fori_loop(0, cap, body, 0, unroll=8)


def _combine(ys, idx, gates, b, t):
    e, _, d = ys.shape
    cap = idx.shape[-1]
    smem = lambda: pl.BlockSpec((1, 1, cap), lambda i, x: (i * e + x, 0, 0), memory_space=pltpu.SMEM)
    return pl.pallas_call(
        functools.partial(_combine_kernel, cap=cap, t=t),
        out_shape=jax.ShapeDtypeStruct((b, t, d), F32),
        grid=(b, e),
        in_specs=[smem(), smem(), pl.BlockSpec((1, cap, d), lambda i, x: (x, i, 0))],
        out_specs=pl.BlockSpec((1, t, d), lambda i, x: (i, 0, 0)),
        compiler_params=_cparams(("parallel", "arbitrary")),
        name="moe_combine",
    )(idx.reshape(b * e, 1, cap), gates.reshape(b * e, 1, cap), ys)


def _ec_moe(uf, lgt, lgc, w_gate, w_up, w_down):
    b, t, d = uf.shape
    e = w_gate.shape[0]
    cap = (EC_CAPACITY_FACTOR * t) // e
    idx, gates = _route(lgt, lgc, cap)
    gidx = (idx + (jnp.arange(b, dtype=I32) * t)[:, None, None]).transpose(1, 0, 2).reshape(e, 1, b * cap)
    ys = _expert_ffn(uf.reshape(b * t, d), gidx, w_gate, w_up, w_down)
    return _combine(ys, idx, gates, b, t)


def kernel(x, c, ctx, c_ctx, w_mod, b_mod, norm_mix_g, norm_ffn_g, gdn_w_in, gdn_conv_w, gdn_a_log, gdn_dt_bias, gdn_o_gain, gdn_w_out, pool_w, pool_scale, moe_w_router, moe_w_gate, moe_w_up, moe_w_down, final_g):
    b, t, d = x.shape
    assert b < MOD_ROWS and t % (2 * CHUNK) == 0 and ctx.shape[1] % CHUNK == 0
    assert d == HEADS * HEAD_DIM and w_mod.shape[0] == 2

    cc = jnp.zeros((MOD_ROWS, d), F32).at[:b].set(c).at[b].set(c_ctx)
    mod = _modulation(cc, w_mod, b_mod)

    def mod_rows(layer, row_slice, bcast):
        parts = []
        for i in range(N_MOD):
            m = mod[layer, row_slice, i * d:(i + 1) * d]
            parts.append(jnp.broadcast_to(m, (b, d)).reshape(b, 1, d) if bcast else m.reshape(b, 1, d))
        return parts

    sh_m, sc_m, gt_m, sh_f, sc_f, gt_f = mod_rows(0, slice(0, b), False)
    csh_m, csc_m = mod_rows(0, slice(b, b + 1), True)[:2]

    w_in = gdn_w_in[0]
    n_main = 4 * HEADS * HEAD_DIM
    w_main = w_in[:, :n_main].astype(BF16)
    w_ab = jnp.pad(w_in[:, n_main:], ((0, 0), (0, LANES - (w_in.shape[1] - n_main))))

    def gate_lanes(v):
        v = jnp.pad(v, ((0, 0), (0, GATE_LANES - HEADS))).reshape(1, N_DIRS * GATE_LANES)
        return jnp.pad(v, ((0, 0), (0, LANES - N_DIRS * GATE_LANES)))

    alog_l = gate_lanes(gdn_a_log[0])
    dtb_l = gate_lanes(gdn_dt_bias[0])
    p_lat, cp_lat = _gdn_project(x, norm_mix_g[0], sh_m, sc_m, w_main, w_ab, alog_l, dtb_l, tm=1024)
    p_ctx, cp_ctx = _gdn_project(ctx, norm_mix_g[0], csh_m, csc_m, w_main, w_ab, alog_l, dtb_l, tm=ctx.shape[1])
    s0 = jnp.zeros((b, HEADS, N_DIRS, HEAD_DIM, HEAD_DIM), F32)
    (s_ctx,) = _gdn_scan(p_ctx, cp_ctx, _row_gates(cp_ctx), gdn_conv_w[0], gdn_o_gain[0], s0, False)
    og, _ = _gdn_scan(p_lat, cp_lat, _row_gates(cp_lat), gdn_conv_w[0], gdn_o_gain[0], s_ctx, True)
    h, uf, lgt, lgc = _ffn_prep(x, norm_ffn_g[0], sh_f, sc_f, moe_w_router[0],
                                proj=(og, gdn_w_out[0].astype(BF16), gt_m))
    moe = _ec_moe(uf, lgt, lgc, moe_w_gate[0], moe_w_up[0], moe_w_down[0])

    sh_m, sc_m, gt_m, sh_f1, sc_f1, gt_f1 = mod_rows(1, slice(0, b), False)
    h, u = _residual(h, moe, gt_f, norm_mix_g[1], sh_m, sc_m)
    h = _pool_mixer(u, h, gt_m, pool_w[0], pool_scale[0])
    uf, lgt, lgc = _ffn_prep(h, norm_ffn_g[1], sh_f1, sc_f1, moe_w_router[1])
    moe = _ec_moe(uf, lgt, lgc, moe_w_gate[1], moe_w_up[1], moe_w_down[1])
    return _residual(h, moe, gt_f1, final_g)
```

```python
import functools

import jax
import jax.numpy as jnp
from jax import lax
from jax.experimental import pallas as pl
from jax.experimental.pallas import tpu as pltpu

F32, BF16, I32 = jnp.float32, jnp.bfloat16, jnp.int32

GRID_W = 64
N_DIRS = 2
HEADS = 8
HEAD_DIM = 128
CONV_W = 5
POOL_WINDOWS = (2, 4, 8, 16)
N_EXPERTS = 16
EC_CAPACITY_FACTOR = 2
N_MOD = 6
RMS_EPS = 1e-6

LANES = 128
SUBLANES = 8
VMEM_LIMIT_BYTES = 58 * 1024 * 1024

CHUNK = 128
MOD_ROWS = 8
GATE_LANES = 16


def _sigmoid(x):
    return 1.0 / (1.0 + jnp.exp(-x))


def _silu(x):
    return x * _sigmoid(x)


def _softplus(x):
    return jnp.maximum(x, 0.0) + jnp.log(1.0 + jnp.exp(-jnp.abs(x)))


def _pieces(x, n):
    out = []
    r = x
    for i in range(n):
        p = r.astype(BF16)
        out.append(p)
        if i + 1 < n:
            r = r - p.astype(F32)
    return out


_NN = (((1,), (0,)), ((), ()))
_NT = (((1,), (1,)), ((), ()))
_TN = (((0,), (0,)), ((), ()))


def _dg(a, b, dims):
    return lax.dot_general(a, b, dims, preferred_element_type=F32)


def _mm(a, b, dims=_NN):
    return _dg(a.astype(BF16), b.astype(BF16), dims)


def _mmp(a, b, na, nb, dims=_NN):
    pa = _pieces(a, na) if na > 1 else [a.astype(BF16)]
    pb = _pieces(b, nb) if nb > 1 else [b.astype(BF16)]
    order = max(na, nb)
    acc = None
    for i, x in enumerate(pa):
        for j, y in enumerate(pb):
            if i + j < order:
                t = _dg(x, y, dims)
                acc = t if acc is None else acc + t
    return acc


def _norm_mod(x, gain, shift, scale):
    ms = jnp.mean(x * x, axis=-1, keepdims=True)
    y = x * lax.rsqrt(ms + RMS_EPS) * gain
    return y * (1.0 + scale) + shift


def _cparams(sem):
    return pltpu.CompilerParams(dimension_semantics=sem, vmem_limit_bytes=VMEM_LIMIT_BYTES)


def _mod_kernel(cc_ref, w_ref, b_ref, o_ref):
    act = _silu(cc_ref[...])
    o_ref[0] = _mmp(act, w_ref[0], 2, 2) + b_ref[0]


def _modulation(cc, w_mod, b_mod):
    depth, d, nd = w_mod.shape
    return pl.pallas_call(
        _mod_kernel,
        out_shape=jax.ShapeDtypeStruct((depth, MOD_ROWS, nd), F32),
        grid=(depth, nd // d),
        in_specs=[
            pl.BlockSpec((MOD_ROWS, d), lambda l, j: (0, 0)),
            pl.BlockSpec((1, d, d), lambda l, j: (l, 0, j)),
            pl.BlockSpec((1, 1, d), lambda l, j: (l, 0, j)),
        ],
        out_specs=pl.BlockSpec((1, MOD_ROWS, d), lambda l, j: (l, 0, j)),
        compiler_params=_cparams(("parallel", "parallel")),
        name="modulation",
    )(cc, w_mod, b_mod.reshape(depth, 1, nd))


def _proj_kernel(x_ref, g_ref, sh_ref, sc_ref, w_ref, wab_ref, alog_ref, dtb_ref,
                 p_ref, cp_ref, u_s, *, tm):
    @pl.when(pl.program_id(2) == 0)
    def _():
        gain = g_ref[...]
        shift = sh_ref[0]
        scale = sc_ref[0]
        lane = lax.broadcasted_iota(I32, (CHUNK, LANES), 1)
        row = lax.broadcasted_iota(I32, (CHUNK, CHUNK), 0)
        col = lax.broadcasted_iota(I32, (CHUNK, CHUNK), 1)
        lower = jnp.where(col <= row, 1.0, 0.0).astype(BF16)
        upper = jnp.where(col >= row, 1.0, 0.0).astype(BF16)
        ones = jnp.ones((CHUNK, CHUNK), BF16)
        is_decay = (lane % GATE_LANES < HEADS) & (lane < N_DIRS * GATE_LANES)

        def body(r, carry):
            rows = pl.ds(pl.multiple_of(r * CHUNK, CHUNK), CHUNK)
            u = _norm_mod(x_ref[0, rows, :], gain, shift, scale)
            u_s[rows, :] = u.astype(BF16)
            ab = _mmp(u, wab_ref[...], 2, 2)
            g = -jnp.exp(alog_ref[...]) * _softplus(ab + dtb_ref[...])
            g = jnp.where(is_decay, g, 0.0)
            beta = _sigmoid(ab)
            gp = _pieces(g, 3)
            gpre = sum(_dg(lower, x, _NN) for x in gp)
            gsuf = sum(_dg(upper, x, _NN) for x in gp)
            gtot = sum(_dg(ones, x, _NN) for x in gp)
            gc = jnp.where(lane < GATE_LANES, gpre, gsuf)
            pk = jnp.where(is_decay, gc, beta)
            pk = jnp.where(lane < N_DIRS * GATE_LANES, pk,
                           pltpu.roll(gtot, N_DIRS * GATE_LANES, axis=1))
            cp_ref[0, rows, :] = jnp.where(lane < 2 * N_DIRS * GATE_LANES, pk, 0.0)
            return carry

        lax.fori_loop(0, tm // CHUNK, body, 0)

    p_ref[0] = jnp.dot(u_s[...], w_ref[...], preferred_element_type=F32)


def _gdn_project(x, gain, shift, scale, w_main, w_ab, alog_l, dtb_l, tm):
    b, t, d = x.shape
    n_main = w_main.shape[1]
    tn = 1024
    return pl.pallas_call(
        functools.partial(_proj_kernel, tm=tm),
        out_shape=(jax.ShapeDtypeStruct((b, t, n_main), F32),
                   jax.ShapeDtypeStruct((b, t, LANES), F32)),
        grid=(b, t // tm, n_main // tn),
        in_specs=[
            pl.BlockSpec((1, tm, d), lambda i, m, j: (i, m, 0)),
            pl.BlockSpec((1, d), lambda i, m, j: (0, 0)),
            pl.BlockSpec((1, 1, d), lambda i, m, j: (i, 0, 0)),
            pl.BlockSpec((1, 1, d), lambda i, m, j: (i, 0, 0)),
            pl.BlockSpec((d, tn), lambda i, m, j: (0, j)),
            pl.BlockSpec((d, LANES), lambda i, m, j: (0, 0)),
            pl.BlockSpec((1, LANES), lambda i, m, j: (0, 0)),
            pl.BlockSpec((1, LANES), lambda i, m, j: (0, 0)),
        ],
        out_specs=(pl.BlockSpec((1, tm, tn), lambda i, m, j: (i, m, j)),
                   pl.BlockSpec((1, tm, LANES), lambda i, m, j: (i, m, 0))),
        scratch_shapes=[pltpu.VMEM((tm, d), BF16)],
        compiler_params=_cparams(("parallel", "parallel", "arbitrary")),
        name="gdn_project",
    )(x, gain.reshape(1, d), shift, scale, w_main, w_ab, alog_l, dtb_l)


def _packed_weights(w):
    hi, lo = _pieces(w, 2)
    return jnp.concatenate([jnp.concatenate([hi, lo], axis=1),
                            jnp.concatenate([hi, jnp.zeros_like(hi)], axis=1)], axis=0)


def _mm_packed(x, wt):
    hi, lo = _pieces(x, 2)
    r = _dg(jnp.concatenate([hi, lo], axis=1), wt, _NN)
    n = wt.shape[1] // 2
    return r[:, :n] + r[:, n:]


def _neumann_inverses(mats, eye):
    ps = [eye - a for a in mats]
    bs = [_mm_packed(a, _packed_weights(a)) for a in mats]
    levels = CHUNK.bit_length() - 1
    for lvl in range(1, levels):
        wts = [_packed_weights(b) for b in bs]
        if lvl + 1 < levels:
            pbs = [_mm_packed(jnp.concatenate([p, b], axis=0), wt) for p, b, wt in zip(ps, bs, wts)]
            ps = [p + pb[:CHUNK] for p, pb in zip(ps, pbs)]
            bs = [pb[CHUNK:] for pb in pbs]
        else:
            ps = [p + _mm_packed(p, wt) for p, wt in zip(ps, wts)]
    return ps


def _gdn_kernel(qp_ref, kp_ref, vp_ref, zp_ref, cp_ref, rp_ref, cwq_ref, cwk_ref, cwv_ref,
                gain_ref, s0_ref, *rest, t, with_out):
    if with_out:
        og_ref, sfin_ref = rest[:2]
        pad_s, q_s, k_s, v_s, of_s, ob_s, st_s, mq_s, n_s, gt_s = rest[2:]
    else:
        (sfin_ref,) = rest[:1]
        pad_s, q_s, k_s, v_s, of_s, ob_s, st_s, mq_s, n_s, gt_s = rest[1:]
    head = pl.program_id(1)
    n_chunks = t // CHUNK
    halo = SUBLANES

    row = lax.broadcasted_iota(I32, (CHUNK, CHUNK), 0)
    col = lax.broadcasted_iota(I32, (CHUNK, CHUNK), 1)

    tile = 256
    zeros_halo = jnp.zeros((halo, HEAD_DIM), F32)
    for src_ref, cw_ref, dst, mode in ((qp_ref, cwq_ref, q_s, "q"), (kp_ref, cwk_ref, k_s, "k"),
                                       (vp_ref, cwv_ref, v_s, "v")):
        pad_s[0:halo, :] = zeros_halo
        pad_s[t + halo:t + 2 * halo, :] = zeros_halo

        def fill(r, carry, src_ref=src_ref):
            rows = pl.multiple_of(r * tile, tile)
            pad_s[pl.ds(rows + halo, tile), :] = src_ref[0, pl.ds(rows, tile), :]
            return carry

        lax.fori_loop(0, t // tile, fill, 0)
        cw = cw_ref[...]

        def conv(r, carry, cw=cw, dst=dst, mode=mode):
            rows = pl.multiple_of(r * tile, tile)
            win = pad_s[pl.ds(rows, tile + 2 * halo), :]
            acc = None
            for j in range(CONV_W):
                off = halo + j - CONV_W // 2
                term = cw[j:j + 1, :] * win[off:off + tile, :]
                acc = term if acc is None else acc + term
            y = _silu(acc)
            if mode != "v":
                y = y * lax.rsqrt(jnp.sum(y * y, axis=-1, keepdims=True) + RMS_EPS)
            if mode == "q":
                y = y * (HEAD_DIM ** -0.5)
            dst[pl.ds(rows, tile), :] = y
            return carry

        lax.fori_loop(0, t // tile, conv, 0)

    st_s[...] = s0_ref[0, 0]
    eye = jnp.where(row == col, 1.0, 0.0)
    masks = ((col <= row, col < row), (col >= row, col > row))
    o_s = (of_s, ob_s)

    group = 4 if n_chunks % 4 == 0 else 2

    def prepare(i, carry):
        chains = []
        for g in range(group):
            c = i * group + g
            rows = pl.ds(pl.multiple_of(c * CHUNK, CHUNK), CHUNK)
            q = q_s[rows, :]
            k = k_s[rows, :]
            v = v_s[rows, :]
            cp = pltpu.roll(cp_ref[0, rows, :], LANES - head, axis=1)
            rp = rp_ref[0, 0, c]
            kq = _mm(jnp.concatenate([k, q], axis=0), k, _NT)
            for d in range(N_DIRS):
                lane = d * GATE_LANES
                bcast = lambda j: jnp.broadcast_to(cp[:, j:j + 1], (CHUNK, HEAD_DIM))
                gc, beta, gtot = bcast(lane), bcast(lane + HEADS), bcast(lane + N_DIRS * GATE_LANES)
                grow = jnp.broadcast_to(rp[d:d + 1, :], (CHUNK, CHUNK))
                incl, strict = masks[d]
                decay = jnp.where(incl, jnp.exp(jnp.where(incl, gc - grow, 0.0)), 0.0)
                e_in = jnp.exp(gc)
                chains.append(dict(
                    d=d, c=c, rows=rows, qd=q * e_in, gtot=gtot,
                    a=jnp.where(strict, beta * kq[:CHUNK] * decay, 0.0),
                    att=kq[CHUNK:] * decay,
                    rhs=jnp.concatenate([v * beta, k * (beta * e_in)], axis=1),
                    kd=k * jnp.exp(gtot - gc)))
        tinvs = _neumann_inverses([ch["a"] for ch in chains], eye)
        uws = [_mm(tinv, ch["rhs"]) for tinv, ch in zip(tinvs, chains)]
        rs = [_mm(jnp.concatenate([ch["att"], ch["kd"].T], axis=0), uw) for ch, uw in zip(chains, uws)]
        for ch, r in zip(chains, rs):
            d, c, rows = ch["d"], ch["c"], ch["rows"]
            mq_s[d, c] = jnp.concatenate([-r[CHUNK:, HEAD_DIM:], ch["qd"] - r[:CHUNK, HEAD_DIM:]],
                                         axis=0).astype(BF16)
            n_s[d, rows, :] = r[CHUNK:, :HEAD_DIM]
            o_s[d][rows, :] = r[:CHUNK, :HEAD_DIM]
            gt_s[d, c] = jnp.exp(ch["gtot"][:SUBLANES])
        return carry

    lax.fori_loop(0, n_chunks // group, prepare, 0)

    def scan(i, carry):
        for d in range(N_DIRS):
            c = i if d == 0 else n_chunks - 1 - i
            rows = pl.ds(pl.multiple_of(c * CHUNK, CHUNK), CHUNK)
            state = st_s[d]
            r = jnp.dot(mq_s[d, c], state.astype(BF16), preferred_element_type=F32)
            gt = jnp.broadcast_to(gt_s[d, c][0:1, :], (HEAD_DIM, HEAD_DIM))
            st_s[d] = state * gt + r[:CHUNK] + n_s[d, rows, :]
            o_s[d][rows, :] = o_s[d][rows, :] + r[CHUNK:]
        return carry

    lax.fori_loop(0, n_chunks, scan, 0, unroll=2)
    sfin_ref[0, 0] = st_s[...]

    if with_out:
        gain = gain_ref[...]

        def emit(r, carry):
            rows = pl.ds(pl.multiple_of(r * tile, tile), tile)
            o = of_s[rows, :] + ob_s[rows, :]
            o = o * lax.rsqrt(jnp.mean(o * o, axis=-1, keepdims=True) + RMS_EPS) * gain
            og_ref[0, rows, :] = (o * _silu(zp_ref[0, rows, :])).astype(BF16)
            return carry

        lax.fori_loop(0, t // tile, emit, 0)


def _gdn_scan(p, cp, rp, conv_w, o_gain, s0, with_out):
    b, t, _ = p.shape
    hd = HEAD_DIM
    col_spec = lambda off: pl.BlockSpec((1, t, hd), lambda i, h, off=off: (i, 0, off + h))
    cw_spec = lambda off: pl.BlockSpec((CONV_W, hd), lambda i, h, off=off: (0, off + h))
    state_spec = pl.BlockSpec((1, 1, N_DIRS, hd, hd), lambda i, h: (i, h, 0, 0, 0))
    out_shape = [jax.ShapeDtypeStruct((b, HEADS, N_DIRS, hd, hd), F32)]
    out_specs = [state_spec]
    if with_out:
        out_shape = [jax.ShapeDtypeStruct((b, t, HEADS * hd), BF16)] + out_shape
        out_specs = [pl.BlockSpec((1, t, hd), lambda i, h: (i, 0, h))] + out_specs
    halo = SUBLANES
    res = pl.pallas_call(
        functools.partial(_gdn_kernel, t=t, with_out=with_out),
        out_shape=tuple(out_shape),
        grid=(b, HEADS),
        in_specs=[
            col_spec(0), col_spec(HEADS), col_spec(2 * HEADS), col_spec(3 * HEADS),
            pl.BlockSpec((1, t, LANES), lambda i, h: (i, 0, 0)),
            pl.BlockSpec((1, 1, t // CHUNK, SUBLANES, CHUNK), lambda i, h: (i, h, 0, 0, 0)),
            cw_spec(0), cw_spec(HEADS), cw_spec(2 * HEADS),
            pl.BlockSpec((1, hd), lambda i, h: (0, 0)),
            state_spec,
        ],
        out_specs=tuple(out_specs),
        scratch_shapes=[
            pltpu.VMEM((t + 2 * halo, hd), F32),
            pltpu.VMEM((t, hd), F32), pltpu.VMEM((t, hd), F32), pltpu.VMEM((t, hd), F32),
            pltpu.VMEM((t, hd), F32), pltpu.VMEM((t, hd), F32),
            pltpu.VMEM((N_DIRS, hd, hd), F32),
            pltpu.VMEM((N_DIRS, t // CHUNK, 2 * CHUNK, hd), BF16),
            pltpu.VMEM((N_DIRS, t, hd), F32),
            pltpu.VMEM((N_DIRS, t // CHUNK, SUBLANES, hd), F32),
        ],
        compiler_params=_cparams(("parallel", "arbitrary")),
        name="gdn_scan_out" if with_out else "gdn_scan_ctx",
    )(p, p, p, p, cp, rp, conv_w, conv_w, conv_w, o_gain.reshape(1, hd), s0)
    return res


def _row_gates(cp):
    b, t, _ = cp.shape
    g = cp[:, :, :N_DIRS * GATE_LANES].reshape(b, t // CHUNK, CHUNK, N_DIRS, GATE_LANES)[..., :HEADS]
    g = g.transpose(0, 4, 1, 3, 2)
    return jnp.pad(g, ((0, 0), (0, 0), (0, 0), (0, SUBLANES - N_DIRS), (0, 0)))


def _ffn_prep_kernel(*refs, tm, with_proj):
    if with_proj:
        (og_ref, wo_ref, x_ref, gt_ref, g_ref, sh_ref, sc_ref, wrt_ref, wrc_ref,
         h_ref, uf_ref, lgt_ref, lgc_ref) = refs
    else:
        (x_ref, g_ref, sh_ref, sc_ref, wrt_ref, wrc_ref, uf_ref, lgt_ref, lgc_ref) = refs
    gain = g_ref[...]
    shift = sh_ref[0]
    scale = sc_ref[0]
    sub = 256

    for r in range(tm // sub):
        rows = slice(r * sub, (r + 1) * sub)
        h = x_ref[0, rows, :]
        if with_proj:
            h = h + gt_ref[0] * jnp.dot(og_ref[0, rows, :], wo_ref[...], preferred_element_type=F32)
            h_ref[0, rows, :] = h
        u = _norm_mod(h, gain, shift, scale)
        uf_ref[0, rows, :] = u
        lgc_ref[0, rows, :] = _mmp(u, wrc_ref[...], 2, 2)
        lgt_ref[0, :, rows] = _mmp(wrt_ref[...], u, 2, 2, _NT)


def _ffn_prep(x, gain, shift, scale, w_router, proj=None):
    b, t, d = x.shape
    e = w_router.shape[1]
    tm = 512
    wrt = w_router.T
    wrc = jnp.pad(w_router, ((0, 0), (0, LANES - e)))
    row_spec = pl.BlockSpec((1, tm, d), lambda i, m: (i, m, 0))
    vec_spec = pl.BlockSpec((1, 1, d), lambda i, m: (i, 0, 0))
    common_in = [pl.BlockSpec((1, d), lambda i, m: (0, 0)), vec_spec, vec_spec,
                 pl.BlockSpec((e, d), lambda i, m: (0, 0)), pl.BlockSpec((d, LANES), lambda i, m: (0, 0))]
    common_args = [gain.reshape(1, d), shift, scale, wrt, wrc]
    outs = [jax.ShapeDtypeStruct((b, t, d), F32), jax.ShapeDtypeStruct((b, e, t), F32),
            jax.ShapeDtypeStruct((b, t, LANES), F32)]
    out_specs = [row_spec, pl.BlockSpec((1, e, tm), lambda i, m: (i, 0, m)),
                 pl.BlockSpec((1, tm, LANES), lambda i, m: (i, m, 0))]
    if proj is not None:
        og, w_out, gate = proj
        in_specs = [row_spec, pl.BlockSpec((d, d), lambda i, m: (0, 0)), row_spec, vec_spec] + common_in
        args = [og, w_out, x, gate] + common_args
        outs = [jax.ShapeDtypeStruct((b, t, d), F32)] + outs
        out_specs = [row_spec] + out_specs
    else:
        in_specs = [row_spec] + common_in
        args = [x] + common_args
    return pl.pallas_call(
        functools.partial(_ffn_prep_kernel, tm=tm, with_proj=proj is not None),
        out_shape=tuple(outs), grid=(b, t // tm), in_specs=in_specs, out_specs=tuple(out_specs),
        compiler_params=_cparams(("parallel", "parallel")),
        name="ffn_prep_proj" if proj is not None else "ffn_prep",
    )(*args)


def _residual_kernel(h_ref, m_ref, gt_ref, g_ref, *refs, tm, mode):
    sub = 256
    gain = g_ref[...]

    def body(r, carry):
        rows = pl.ds(pl.multiple_of(r * sub, sub), sub)
        h = h_ref[0, rows, :] + gt_ref[0] * m_ref[0, rows, :]
        if mode == "mix":
            sh_ref, sc_ref, ho_ref, u_ref = refs
            ho_ref[0, rows, :] = h
            u_ref[0, rows, :] = _norm_mod(h, gain, sh_ref[0], sc_ref[0])
        else:
            (o_ref,) = refs
            ms = jnp.mean(h * h, axis=-1, keepdims=True)
            o_ref[0, rows, :] = h * lax.rsqrt(ms + RMS_EPS) * gain
        return carry

    lax.fori_loop(0, tm // sub, body, 0)


def _residual(h, m, gate, gain, shift=None, scale=None):
    b, t, d = h.shape
    tm = 512
    row_spec = pl.BlockSpec((1, tm, d), lambda i, j: (i, j, 0))
    vec_spec = pl.BlockSpec((1, 1, d), lambda i, j: (i, 0, 0))
    in_specs = [row_spec, row_spec, vec_spec, pl.BlockSpec((1, d), lambda i, j: (0, 0))]
    args = [h, m, gate, gain.reshape(1, d)]
    if shift is not None:
        mode = "mix"
        in_specs += [vec_spec, vec_spec]
        args += [shift, scale]
        outs = (jax.ShapeDtypeStruct((b, t, d), F32), jax.ShapeDtypeStruct((b, t, d), F32))
        out_specs = (row_spec, row_spec)
    else:
        mode = "final"
        outs = jax.ShapeDtypeStruct((b, t, d), F32)
        out_specs = row_spec
    return pl.pallas_call(
        functools.partial(_residual_kernel, tm=tm, mode=mode),
        out_shape=outs, grid=(b, t // tm), in_specs=in_specs, out_specs=out_specs,
        compiler_params=_cparams(("parallel", "parallel")),
        name="residual_" + mode,
    )(*args)


def _pool_kernel(u_ref, h_ref, gt_ref, w_ref, sc_ref, o_ref, m1_s, *, t):
    group = pl.program_id(1)
    tile = 256
    rows_per_tile = tile // GRID_W
    n_rows = t // GRID_W
    halo = max(POOL_WINDOWS) // 2 * GRID_W
    ti = lax.broadcasted_iota(I32, (tile, tile), 0)
    si = lax.broadcasted_iota(I32, (tile, tile), 1)
    tcol = lax.broadcasted_iota(I32, (tile, 1), 0)

    def run(win):
        lo = win // 2
        hi = win - lo
        off = si - ti
        band = jnp.where((ti // GRID_W == si // GRID_W) & (off >= -lo) & (off < hi), 1.0, 0.0).astype(BF16)
        cpos = tcol % GRID_W
        inv_c = 1.0 / (jnp.minimum(cpos + hi, GRID_W) - jnp.maximum(cpos - lo, 0)).astype(F32)
        m1_s[0:halo, :] = jnp.zeros((halo, m1_s.shape[1]), F32)
        m1_s[halo + t:2 * halo + t, :] = jnp.zeros((halo, m1_s.shape[1]), F32)

        def along_w(r, carry):
            rows = pl.multiple_of(r * tile, tile)
            x = u_ref[0, pl.ds(rows, tile), :]
            tot = sum(_dg(band, p, _NN) for p in _pieces(x, 3))
            m1_s[pl.ds(rows + halo, tile), :] = tot * inv_c
            return carry

        lax.fori_loop(0, t // tile, along_w, 0)
        w = w_ref[0]
        scale = sc_ref[...]
        gate = gt_ref[0]

        def along_h(r, carry):
            rows = pl.multiple_of(r * tile, tile)
            acc = None
            for o in range(-lo, hi):
                term = m1_s[pl.ds(rows + halo + o * GRID_W, tile), :]
                acc = term if acc is None else acc + term
            rpos = r * rows_per_tile + tcol // GRID_W
            inv_r = 1.0 / (jnp.minimum(rpos + hi, n_rows) - jnp.maximum(rpos - lo, 0)).astype(F32)
            x = u_ref[0, pl.ds(rows, tile), :]
            y = _mmp(acc * inv_r - x, w, 2, 2) * scale
            o_ref[0, pl.ds(rows, tile), :] = h_ref[0, pl.ds(rows, tile), :] + gate * y
            return carry

        lax.fori_loop(0, t // tile, along_h, 0)

    for gi, win in enumerate(POOL_WINDOWS):
        pl.when(group == gi)(functools.partial(run, win))


def _pool_mixer(u, h, gate, w_grp, scale):
    b, t, d = u.shape
    groups = len(POOL_WINDOWS)
    gd = d // groups
    halo = max(POOL_WINDOWS) // 2 * GRID_W
    blk = pl.BlockSpec((1, t, gd), lambda i, g: (i, 0, g))
    return pl.pallas_call(
        functools.partial(_pool_kernel, t=t),
        out_shape=jax.ShapeDtypeStruct((b, t, d), F32),
        grid=(b, groups),
        in_specs=[blk, blk, pl.BlockSpec((1, 1, gd), lambda i, g: (i, 0, g)),
                  pl.BlockSpec((1, gd, gd), lambda i, g: (g, 0, 0)),
                  pl.BlockSpec((1, gd), lambda i, g: (0, g))],
        out_specs=blk,
        scratch_shapes=[pltpu.VMEM((t + 2 * halo, gd), F32)],
        compiler_params=_cparams(("parallel", "parallel")),
        name="pool_mixer",
    )(u, h, gate, w_grp, scale.reshape(1, d))


def _exclusive_count(mask, tri, ones):
    n = mask.shape[1] // LANES
    run = jnp.zeros((mask.shape[0], LANES), F32)
    maskf = jnp.where(mask, 1.0, 0.0)
    tiles = []
    for j in range(n):
        m = maskf[:, j * LANES:(j + 1) * LANES].astype(BF16)
        tiles.append(_dg(m, tri, _NN) + run)
        run = run + _dg(m, ones, _NN)
    return jnp.concatenate(tiles, axis=1), run


def _route_kernel(lgt_ref, lgc_ref, o_ref, slot_s, rhs_s, *, t, cap, tk):
    e = lgt_ref.shape[1]
    lg = lgt_ref[0]
    ex = jnp.exp(lg - jnp.max(lg, axis=0, keepdims=True))
    aff = ex / jnp.sum(ex, axis=0, keepdims=True)

    def count_ge(thr):
        return jnp.sum(jnp.where(aff >= thr, 1.0, 0.0), axis=1, keepdims=True)

    n_exp = 126.0

    def pow2(k):
        return jnp.where(k >= n_exp, 0.0, jnp.exp2(-k))

    def search_exponent(i, carry):
        k_hi, k_lo = carry
        k_mid = jnp.floor(0.5 * (k_hi + k_lo))
        ok = count_ge(pow2(k_mid)) >= cap
        return jnp.where(ok, k_hi, k_mid), jnp.where(ok, k_mid, k_lo)

    k_hi, k_lo = lax.fori_loop(0, 8, search_exponent,
                               (jnp.full((e, 1), -1.0, F32), jnp.full((e, 1), n_exp, F32)))

    def search_value(i, carry):
        lo, hi = carry
        mid = 0.5 * (lo + hi)
        ok = count_ge(mid) >= cap
        return jnp.where(ok, mid, lo), jnp.where(ok, hi, mid)

    lo, hi = lax.fori_loop(0, 26, search_value, (pow2(k_lo), pow2(k_hi)))
    above = aff >= hi
    equal = (aff >= lo) & (aff < hi)
    ri = lax.broadcasted_iota(I32, (LANES, LANES), 0)
    ci = lax.broadcasted_iota(I32, (LANES, LANES), 1)
    tri = jnp.where(ri < ci, 1.0, 0.0).astype(BF16)
    ones = jnp.ones((LANES, LANES), BF16)
    n_above = jnp.sum(jnp.where(above, 1.0, 0.0), axis=1, keepdims=True)
    eq_rank, _ = _exclusive_count(equal, tri, ones)
    sel = above | (equal & (eq_rank < cap - n_above))
    slot, _ = _exclusive_count(sel, tri, ones)
    slot = jnp.where(sel, slot, -1.0)
    per = tk // LANES
    for j in range(t // LANES):
        for x in range(e):
            slot_s[j // per, x, :, (j % per) * LANES:(j % per + 1) * LANES] = slot[x:x + 1, j * LANES:(j + 1) * LANES]

    lane = lax.broadcasted_iota(I32, (tk, LANES), 1)
    tok0 = lax.broadcasted_iota(I32, (tk, LANES), 0)

    def build(r, carry):
        rows = pl.ds(pl.multiple_of(r * tk, tk), tk)
        lc = jnp.where(lane < e, lgc_ref[0, rows, :], -jnp.inf)
        exc = jnp.exp(lc - jnp.max(lc, axis=1, keepdims=True))
        affc = exc / jnp.sum(exc, axis=1, keepdims=True)
        p0, p1, p2 = [p.astype(F32) for p in _pieces(affc, 3)]
        packed = (pltpu.roll(p0, 2, axis=1) + pltpu.roll(p1, 2 + e, axis=1)
                  + pltpu.roll(p2, 2 + 2 * e, axis=1))
        tok = tok0 + r * tk
        packed = jnp.where(lane == 0, (tok // GRID_W).astype(F32),
                           jnp.where(lane == 1, (tok % GRID_W).astype(F32), packed))
        rhs_s[rows, :] = packed.astype(BF16)
        return carry

    lax.fori_loop(0, t // tk, build, 0)

    s_iota = lax.broadcasted_iota(I32, (cap, tk), 0).astype(F32)

    def compact(i, carry):
        x = i // (t // tk)
        j = i % (t // tk)
        onehot = jnp.where(s_iota == slot_s[j, x], 1.0, 0.0).astype(BF16)
        part = _dg(onehot, rhs_s[pl.ds(pl.multiple_of(j * tk, tk), tk), :], _NN)

        @pl.when(j == 0)
        def _():
            o_ref[0, x] = part

        @pl.when(j != 0)
        def _():
            o_ref[0, x] = o_ref[0, x] + part

        return carry

    lax.fori_loop(0, e * (t // tk), compact, 0)


def _route(lgt, lgc, cap):
    b, e, t = lgt.shape
    tk = 512
    packed = pl.pallas_call(
        functools.partial(_route_kernel, t=t, cap=cap, tk=tk),
        out_shape=jax.ShapeDtypeStruct((b, e, cap, LANES), F32),
        grid=(b,),
        in_specs=[pl.BlockSpec((1, e, t), lambda i: (i, 0, 0)),
                  pl.BlockSpec((1, t, LANES), lambda i: (i, 0, 0))],
        out_specs=pl.BlockSpec((1, e, cap, LANES), lambda i: (i, 0, 0, 0)),
        scratch_shapes=[pltpu.VMEM((t // tk, e, 1, tk), F32), pltpu.VMEM((t, LANES), BF16)],
        compiler_params=_cparams(("parallel",)),
        name="route_topk",
    )(lgt, lgc)
    idx = (packed[..., 0] * GRID_W + packed[..., 1]).astype(I32)
    pieces = packed[..., 2:2 + 3 * e].reshape(b, e, cap, 3, e).sum(axis=3)
    gates = jnp.take_along_axis(pieces, jnp.arange(e).reshape(1, e, 1, 1), axis=3)[..., 0]
    return idx, gates


def _ffn_kernel(idx_ref, idx_next_ref, uf_hbm, wg_ref, wu_ref, wd_ref, ys_ref,
                x_s, xb_s, wg_s, wu_s, wd_s, sem, *, rows, tm):
    x, f = pl.program_id(0), pl.program_id(1)
    n_x, n_f = pl.num_programs(0), pl.num_programs(1)
    slot = x % 2
    n_tiles = rows // tm
    per_tile = rows // (n_f * n_tiles)

    def row_copy(token, r, s):
        return pltpu.make_async_copy(uf_hbm.at[pl.ds(token, 1), :], x_s.at[s, pl.ds(r, 1), :], sem.at[s])

    def all_rows(s):
        return pltpu.make_async_copy(uf_hbm.at[pl.ds(0, rows), :], x_s.at[s], sem.at[s])

    @pl.when((x == 0) & (f == 0))
    def _():
        def issue(r, carry):
            row_copy(idx_ref[0, 0, r], r, 0).start()
            return carry

        lax.fori_loop(0, rows, issue, 0, unroll=8)

    @pl.when(f == 0)
    def _():
        all_rows(slot).wait()

        def cast(r, carry):
            rr = pl.ds(pl.multiple_of(r * tm, tm), tm)
            xb_s[rr, :] = x_s[slot, rr, :].astype(BF16)
            return carry

        lax.fori_loop(0, n_tiles, cast, 0)

    wg_s[...] = wg_ref[0, 0].astype(BF16)
    wu_s[...] = wu_ref[0, 0].astype(BF16)
    wd_s[...] = wd_ref[0, 0].astype(BF16)

    def body(m, carry):
        base = (f * n_tiles + m) * per_tile
        for j in range(per_tile):
            row_copy(idx_next_ref[0, 0, base + j], base + j, 1 - slot).start()
        rr = pl.ds(pl.multiple_of(m * tm, tm), tm)
        xt = xb_s[rr, :]
        g = jnp.dot(xt, wg_s[...], preferred_element_type=F32)
        u = jnp.dot(xt, wu_s[...], preferred_element_type=F32)
        hid = (_silu(g) * u).astype(BF16)
        y = jnp.dot(hid, wd_s[...], preferred_element_type=F32)

        @pl.when(f == 0)
        def _():
            ys_ref[0, rr, :] = y

        @pl.when(f != 0)
        def _():
            ys_ref[0, rr, :] = ys_ref[0, rr, :] + y

        return carry

    lax.fori_loop(0, n_tiles, body, 0)

    @pl.when((x == n_x - 1) & (f == n_f - 1))
    def _():
        all_rows(1 - slot).wait()


def _expert_ffn(uf_flat, gidx, w_gate, w_up, w_down, layer):
    _, e, d, de = w_gate.shape
    rows = gidx.shape[-1]
    tf = 512
    tm = 512
    assert rows % tm == 0 and rows % ((rows // tm) * (de // tf)) == 0
    return pl.pallas_call(
        functools.partial(_ffn_kernel, rows=rows, tm=tm),
        out_shape=jax.ShapeDtypeStruct((e, rows, d), F32),
        grid=(e, de // tf),
        in_specs=[
            pl.BlockSpec((1, 1, rows), lambda x, f: (x, 0, 0), memory_space=pltpu.SMEM),
            pl.BlockSpec((1, 1, rows), lambda x, f: (jnp.minimum(x + 1, e - 1), 0, 0), memory_space=pltpu.SMEM),
            pl.BlockSpec(memory_space=pl.ANY),
            pl.BlockSpec((1, 1, d, tf), lambda x, f: (layer, x, 0, f)),
            pl.BlockSpec((1, 1, d, tf), lambda x, f: (layer, x, 0, f)),
            pl.BlockSpec((1, 1, tf, d), lambda x, f: (layer, x, f, 0)),
        ],
        out_specs=pl.BlockSpec((1, rows, d), lambda x, f: (x, 0, 0)),
        scratch_shapes=[pltpu.VMEM((2, rows, d), F32), pltpu.VMEM((rows, d), BF16),
                        pltpu.VMEM((d, tf), BF16), pltpu.VMEM((d, tf), BF16), pltpu.VMEM((tf, d), BF16),
                        pltpu.SemaphoreType.DMA((2,))],
        compiler_params=_cparams(("arbitrary", "arbitrary")),
        name="expert_ffn",
    )(gidx, gidx, uf_flat, w_gate, w_up, w_down)


def _combine_kernel(idx_ref, gate_ref, ys_ref, o_ref, *, cap, t):
    @pl.when(pl.program_id(1) == 0)
    def _():
        tile = 256

        def clear(r, carry):
            o_ref[0, pl.ds(pl.multiple_of(r * tile, tile), tile), :] = jnp.zeros((tile, o_ref.shape[2]), F32)
            return carry

        lax.fori_loop(0, t // tile, clear, 0)

    def body(s, carry):
        tok = idx_ref[0, 0, s]
        o_ref[0, pl.ds(tok, 1), :] = o_ref[0, pl.ds(tok, 1), :] + gate_ref[0, 0, s] * ys_ref[0, pl.ds(s, 1), :]
        return carry

    lax.fori_loop(0, cap, body, 0, unroll=8)


def _combine(ys, idx, gates, b, t):
    e, _, d = ys.shape
    cap = idx.shape[-1]
    smem = lambda: pl.BlockSpec((1, 1, cap), lambda i, x: (i * e + x, 0, 0), memory_space=pltpu.SMEM)
    return pl.pallas_call(
        functools.partial(_combine_kernel, cap=cap, t=t),
        out_shape=jax.ShapeDtypeStruct((b, t, d), F32),
        grid=(b, e),
        in_specs=[smem(), smem(), pl.BlockSpec((1, cap, d), lambda i, x: (x, i, 0))],
        out_specs=pl.BlockSpec((1, t, d), lambda i, x: (i, 0, 0)),
        compiler_params=_cparams(("parallel", "arbitrary")),
        name="moe_combine",
    )(idx.reshape(b * e, 1, cap), gates.reshape(b * e, 1, cap), ys)


def _ec_moe(uf, lgt, lgc, w_gate, w_up, w_down, layer):
    b, t, d = uf.shape
    e = w_gate.shape[1]
    cap = (EC_CAPACITY_FACTOR * t) // e
    idx, gates = _route(lgt, lgc, cap)
    gidx = (idx + (jnp.arange(b, dtype=I32) * t)[:, None, None]).transpose(1, 0, 2).reshape(e, 1, b * cap)
    ys = _expert_ffn(uf.reshape(b * t, d), gidx, w_gate, w_up, w_down, layer)
    return _combine(ys, idx, gates, b, t)


def kernel(x, c, ctx, c_ctx, w_mod, b_mod, norm_mix_g, norm_ffn_g, gdn_w_in, gdn_conv_w, gdn_a_log, gdn_dt_bias, gdn_o_gain, gdn_w_out, pool_w, pool_scale, moe_w_router, moe_w_gate, moe_w_up, moe_w_down, final_g):
    b, t, d = x.shape
    assert b < MOD_ROWS and t % (2 * CHUNK) == 0 and ctx.shape[1] % (2 * CHUNK) == 0
    assert d == HEADS * HEAD_DIM and w_mod.shape[0] == 2

    cc = jnp.zeros((MOD_ROWS, d), F32).at[:b].set(c).at[b].set(c_ctx)
    mod = _modulation(cc, w_mod, b_mod)

    def mod_rows(layer, row_slice, bcast):
        parts = []
        for i in range(N_MOD):
            m = mod[layer, row_slice, i * d:(i + 1) * d]
            parts.append(jnp.broadcast_to(m, (b, d)).reshape(b, 1, d) if bcast else m.reshape(b, 1, d))
        return parts

    sh_m, sc_m, gt_m, sh_f, sc_f, gt_f = mod_rows(0, slice(0, b), False)
    csh_m, csc_m = mod_rows(0, slice(b, b + 1), True)[:2]

    w_in = gdn_w_in[0]
    n_main = 4 * HEADS * HEAD_DIM
    w_main = w_in[:, :n_main].astype(BF16)
    w_ab = jnp.pad(w_in[:, n_main:], ((0, 0), (0, LANES - (w_in.shape[1] - n_main))))

    def gate_lanes(v):
        v = jnp.pad(v, ((0, 0), (0, GATE_LANES - HEADS))).reshape(1, N_DIRS * GATE_LANES)
        return jnp.pad(v, ((0, 0), (0, LANES - N_DIRS * GATE_LANES)))

    alog_l = gate_lanes(gdn_a_log[0])
    dtb_l = gate_lanes(gdn_dt_bias[0])
    p_lat, cp_lat = _gdn_project(x, norm_mix_g[0], sh_m, sc_m, w_main, w_ab, alog_l, dtb_l, tm=1024)
    p_ctx, cp_ctx = _gdn_project(ctx, norm_mix_g[0], csh_m, csc_m, w_main, w_ab, alog_l, dtb_l, tm=ctx.shape[1])
    s0 = jnp.zeros((b, HEADS, N_DIRS, HEAD_DIM, HEAD_DIM), F32)
    (s_ctx,) = _gdn_scan(p_ctx, cp_ctx, _row_gates(cp_ctx), gdn_conv_w[0], gdn_o_gain[0], s0, False)
    og, _ = _gdn_scan(p_lat, cp_lat, _row_gates(cp_lat), gdn_conv_w[0], gdn_o_gain[0], s_ctx, True)
    h, uf, lgt, lgc = _ffn_prep(x, norm_ffn_g[0], sh_f, sc_f, moe_w_router[0],
                                proj=(og, gdn_w_out[0].astype(BF16), gt_m))
    moe = _ec_moe(uf, lgt, lgc, moe_w_gate, moe_w_up, moe_w_down, 0)

    sh_m, sc_m, gt_m, sh_f1, sc_f1, gt_f1 = mod_rows(1, slice(0, b), False)
    h, u = _residual(h, moe, gt_f, norm_mix_g[1], sh_m, sc_m)
    h = _pool_mixer(u, h, gt_m, pool_w[0], pool_scale[0])
    uf, lgt, lgc = _ffn_prep(h, norm_ffn_g[1], sh_f1, sc_f1, moe_w_router[1])
    moe = _ec_moe(uf, lgt, lgc, moe_w_gate, moe_w_up, moe_w_down, 1)
    return _residual(h, moe, gt_f1, final_g)
```

```python
import functools

import jax
import jax.numpy as jnp
from jax import lax
from jax.experimental import pallas as pl
from jax.experimental.pallas import tpu as pltpu

F32, BF16, I32 = jnp.float32, jnp.bfloat16, jnp.int32

GRID_W = 64
N_DIRS = 2
HEADS = 8
HEAD_DIM = 128
CONV_W = 5
POOL_WINDOWS = (2, 4, 8, 16)
N_EXPERTS = 16
EC_CAPACITY_FACTOR = 2
N_MOD = 6
RMS_EPS = 1e-6

LANES = 128
SUBLANES = 8
VMEM_LIMIT_BYTES = 58 * 1024 * 1024

CHUNK = 128
MOD_ROWS = 8
GATE_LANES = 16
ROUTE_ROWS = 64
INDEX_BASE = 64


def _sigmoid(x):
    return 1.0 / (1.0 + jnp.exp(-x))


def _silu(x):
    return x * _sigmoid(x)


def _softplus(x):
    return jnp.maximum(x, 0.0) + jnp.log(1.0 + jnp.exp(-jnp.abs(x)))


def _pieces(x, n):
    out = []
    r = x
    for i in range(n):
        p = r.astype(BF16)
        out.append(p)
        if i + 1 < n:
            r = r - p.astype(F32)
    return out


_NN = (((1,), (0,)), ((), ()))
_NT = (((1,), (1,)), ((), ()))
_TN = (((0,), (0,)), ((), ()))


def _dg(a, b, dims):
    return lax.dot_general(a, b, dims, preferred_element_type=F32)


def _mm(a, b, dims=_NN):
    return _dg(a.astype(BF16), b.astype(BF16), dims)


def _mmp(a, b, na, nb, dims=_NN):
    pa = _pieces(a, na) if na > 1 else [a.astype(BF16)]
    pb = _pieces(b, nb) if nb > 1 else [b.astype(BF16)]
    order = max(na, nb)
    acc = None
    for i, x in enumerate(pa):
        for j, y in enumerate(pb):
            if i + j < order:
                t = _dg(x, y, dims)
                acc = t if acc is None else acc + t
    return acc


def _norm_mod(x, gain, shift, scale):
    ms = jnp.mean(x * x, axis=-1, keepdims=True)
    y = x * lax.rsqrt(ms + RMS_EPS) * gain
    return y * (1.0 + scale) + shift


def _cparams(sem):
    return pltpu.CompilerParams(dimension_semantics=sem, vmem_limit_bytes=VMEM_LIMIT_BYTES)


def _mod_kernel(cc_ref, w_ref, b_ref, o_ref):
    act = _silu(cc_ref[...])
    o_ref[0] = _mmp(act, w_ref[0], 2, 2) + b_ref[0]


def _modulation(cc, w_mod, b_mod):
    depth, d, nd = w_mod.shape
    return pl.pallas_call(
        _mod_kernel,
        out_shape=jax.ShapeDtypeStruct((depth, MOD_ROWS, nd), F32),
        grid=(depth, nd // d),
        in_specs=[
            pl.BlockSpec((MOD_ROWS, d), lambda l, j: (0, 0)),
            pl.BlockSpec((1, d, d), lambda l, j: (l, 0, j)),
            pl.BlockSpec((1, 1, d), lambda l, j: (l, 0, j)),
        ],
        out_specs=pl.BlockSpec((1, MOD_ROWS, d), lambda l, j: (l, 0, j)),
        compiler_params=_cparams(("parallel", "parallel")),
        name="modulation",
    )(cc, w_mod, b_mod.reshape(depth, 1, nd))


def _proj_kernel(x_ref, g_ref, sh_ref, sc_ref, w_ref, wab_ref, alog_ref, dtb_ref,
                 p_ref, cp_ref, u_s, *, tm):
    @pl.when(pl.program_id(2) == 0)
    def _():
        gain = g_ref[...]
        shift = sh_ref[0]
        scale = sc_ref[0]
        lane = lax.broadcasted_iota(I32, (CHUNK, LANES), 1)
        row = lax.broadcasted_iota(I32, (CHUNK, CHUNK), 0)
        col = lax.broadcasted_iota(I32, (CHUNK, CHUNK), 1)
        lower = jnp.where(col <= row, 1.0, 0.0).astype(BF16)
        upper = jnp.where(col >= row, 1.0, 0.0).astype(BF16)
        ones = jnp.ones((CHUNK, CHUNK), BF16)
        is_decay = (lane % GATE_LANES < HEADS) & (lane < N_DIRS * GATE_LANES)

        def body(r, carry):
            rows = pl.ds(pl.multiple_of(r * CHUNK, CHUNK), CHUNK)
            u = _norm_mod(x_ref[0, rows, :], gain, shift, scale)
            u_s[rows, :] = u.astype(BF16)
            ab = _mmp(u, wab_ref[...], 2, 2)
            g = -jnp.exp(alog_ref[...]) * _softplus(ab + dtb_ref[...])
            g = jnp.where(is_decay, g, 0.0)
            beta = _sigmoid(ab)
            gp = _pieces(g, 3)
            gpre = sum(_dg(lower, x, _NN) for x in gp)
            gsuf = sum(_dg(upper, x, _NN) for x in gp)
            gtot = sum(_dg(ones, x, _NN) for x in gp)
            gc = jnp.where(lane < GATE_LANES, gpre, gsuf)
            pk = jnp.where(is_decay, gc, beta)
            pk = jnp.where(lane < N_DIRS * GATE_LANES, pk,
                           pltpu.roll(gtot, N_DIRS * GATE_LANES, axis=1))
            cp_ref[0, rows, :] = jnp.where(lane < 2 * N_DIRS * GATE_LANES, pk, 0.0)
            return carry

        lax.fori_loop(0, tm // CHUNK, body, 0)

    p_ref[0] = jnp.dot(u_s[...], w_ref[...], preferred_element_type=F32)


def _gdn_project(x, gain, shift, scale, w_main, w_ab, alog_l, dtb_l, tm):
    b, t, d = x.shape
    n_main = w_main.shape[1]
    tn = 1024
    return pl.pallas_call(
        functools.partial(_proj_kernel, tm=tm),
        out_shape=(jax.ShapeDtypeStruct((b, t, n_main), F32),
                   jax.ShapeDtypeStruct((b, t, LANES), F32)),
        grid=(b, t // tm, n_main // tn),
        in_specs=[
            pl.BlockSpec((1, tm, d), lambda i, m, j: (i, m, 0)),
            pl.BlockSpec((1, d), lambda i, m, j: (0, 0)),
            pl.BlockSpec((1, 1, d), lambda i, m, j: (i, 0, 0)),
            pl.BlockSpec((1, 1, d), lambda i, m, j: (i, 0, 0)),
            pl.BlockSpec((d, tn), lambda i, m, j: (0, j)),
            pl.BlockSpec((d, LANES), lambda i, m, j: (0, 0)),
            pl.BlockSpec((1, LANES), lambda i, m, j: (0, 0)),
            pl.BlockSpec((1, LANES), lambda i, m, j: (0, 0)),
        ],
        out_specs=(pl.BlockSpec((1, tm, tn), lambda i, m, j: (i, m, j)),
                   pl.BlockSpec((1, tm, LANES), lambda i, m, j: (i, m, 0))),
        scratch_shapes=[pltpu.VMEM((tm, d), BF16)],
        compiler_params=_cparams(("parallel", "parallel", "arbitrary")),
        name="gdn_project",
    )(x, gain.reshape(1, d), shift, scale, w_main, w_ab, alog_l, dtb_l)


def _packed_weights(w):
    hi, lo = _pieces(w, 2)
    return jnp.concatenate([jnp.concatenate([hi, lo], axis=1),
                            jnp.concatenate([hi, jnp.zeros_like(hi)], axis=1)], axis=0)


def _mm_packed(x, wt):
    hi, lo = _pieces(x, 2)
    r = _dg(jnp.concatenate([hi, lo], axis=1), wt, _NN)
    n = wt.shape[1] // 2
    return r[:, :n] + r[:, n:]


def _neumann_inverses(mats, eye):
    ps = [eye - a for a in mats]
    bs = [_mm_packed(a, _packed_weights(a)) for a in mats]
    levels = CHUNK.bit_length() - 1
    for lvl in range(1, levels):
        wts = [_packed_weights(b) for b in bs]
        if lvl + 1 < levels:
            pbs = [_mm_packed(jnp.concatenate([p, b], axis=0), wt) for p, b, wt in zip(ps, bs, wts)]
            ps = [p + pb[:CHUNK] for p, pb in zip(ps, pbs)]
            bs = [pb[CHUNK:] for pb in pbs]
        else:
            ps = [p + _mm_packed(p, wt) for p, wt in zip(ps, wts)]
    return ps


def _gdn_kernel(qp_ref, kp_ref, vp_ref, zp_ref, cp_ref, rp_ref, cwq_ref, cwk_ref, cwv_ref,
                gain_ref, s0_ref, *rest, t, with_out):
    if with_out:
        og_ref, sfin_ref = rest[:2]
        pad_s, q_s, k_s, v_s, of_s, ob_s, st_s, mq_s, n_s, gt_s = rest[2:]
    else:
        (sfin_ref,) = rest[:1]
        pad_s, q_s, k_s, v_s, of_s, ob_s, st_s, mq_s, n_s, gt_s = rest[1:]
    head = pl.program_id(1)
    n_chunks = t // CHUNK
    halo = SUBLANES

    row = lax.broadcasted_iota(I32, (CHUNK, CHUNK), 0)
    col = lax.broadcasted_iota(I32, (CHUNK, CHUNK), 1)

    tile = 256
    zeros_halo = jnp.zeros((halo, HEAD_DIM), F32)
    streams = ((qp_ref, cwq_ref, q_s, "q"), (kp_ref, cwk_ref, k_s, "k"), (vp_ref, cwv_ref, v_s, "v"))
    for n, (src_ref, _, _, _) in enumerate(streams):
        pad_s[n, 0:halo, :] = zeros_halo
        pad_s[n, t + halo:t + 2 * halo, :] = zeros_halo

        def fill(r, carry, src_ref=src_ref, n=n):
            rows = pl.multiple_of(r * tile, tile)
            pad_s[n, pl.ds(rows + halo, tile), :] = src_ref[0, pl.ds(rows, tile), :]
            return carry

        lax.fori_loop(0, t // tile, fill, 0)
    conv_w = [cw_ref[...] for _, cw_ref, _, _ in streams]

    def conv_tiles(first, count):
        for tt in range(count):
            rows = pl.multiple_of((first + tt) * tile, tile)
            for n, (_, _, dst, mode) in enumerate(streams):
                win = pad_s[n, pl.ds(rows, tile + 2 * halo), :]
                acc = None
                for j in range(CONV_W):
                    off = halo + j - CONV_W // 2
                    term = conv_w[n][j:j + 1, :] * win[off:off + tile, :]
                    acc = term if acc is None else acc + term
                y = _silu(acc)
                if mode != "v":
                    y = y * lax.rsqrt(jnp.sum(y * y, axis=-1, keepdims=True) + RMS_EPS)
                if mode == "q":
                    y = y * (HEAD_DIM ** -0.5)
                dst[pl.ds(rows, tile), :] = y

    st_s[...] = s0_ref[0, 0]
    eye = jnp.where(row == col, 1.0, 0.0)
    masks = ((col <= row, col < row), (col >= row, col > row))
    o_s = (of_s, ob_s)

    group = 4 if n_chunks % 4 == 0 else 2

    def prepare(i):
        chains = []
        for g in range(group):
            c = i * group + g
            rows = pl.ds(pl.multiple_of(c * CHUNK, CHUNK), CHUNK)
            q = q_s[rows, :]
            k = k_s[rows, :]
            v = v_s[rows, :]
            cp = pltpu.roll(cp_ref[0, rows, :], LANES - head, axis=1)
            rp = rp_ref[0, 0, c]
            kq = _mm(jnp.concatenate([k, q], axis=0), k, _NT)
            for d in range(N_DIRS):
                lane = d * GATE_LANES
                bcast = lambda j: jnp.broadcast_to(cp[:, j:j + 1], (CHUNK, HEAD_DIM))
                gc, beta, gtot = bcast(lane), bcast(lane + HEADS), bcast(lane + N_DIRS * GATE_LANES)
                grow = jnp.broadcast_to(rp[d:d + 1, :], (CHUNK, CHUNK))
                incl, strict = masks[d]
                decay = jnp.where(incl, jnp.exp(jnp.where(incl, gc - grow, 0.0)), 0.0)
                e_in = jnp.exp(gc)
                chains.append(dict(
                    d=d, c=c, rows=rows, qd=q * e_in, gtot=gtot,
                    a=jnp.where(strict, beta * kq[:CHUNK] * decay, 0.0),
                    att=kq[CHUNK:] * decay,
                    rhs=jnp.concatenate([v * beta, k * (beta * e_in)], axis=1),
                    kd=k * jnp.exp(gtot - gc)))
        tinvs = _neumann_inverses([ch["a"] for ch in chains], eye)
        uws = [_mm(tinv, ch["rhs"]) for tinv, ch in zip(tinvs, chains)]
        rs = [_mm(jnp.concatenate([ch["att"], ch["kd"].T], axis=0), uw) for ch, uw in zip(chains, uws)]
        for ch, r in zip(chains, rs):
            d, c, rows = ch["d"], ch["c"], ch["rows"]
            mq_s[d, c] = jnp.concatenate([-r[CHUNK:, HEAD_DIM:], ch["qd"] - r[:CHUNK, HEAD_DIM:]],
                                         axis=0).astype(BF16)
            n_s[d, rows, :] = r[CHUNK:, :HEAD_DIM]
            o_s[d][rows, :] = r[:CHUNK, :HEAD_DIM]
            gt_s[d, c] = jnp.exp(ch["gtot"][:SUBLANES])

    n_groups = n_chunks // group
    tiles_per_group = group * CHUNK // tile
    conv_tiles(0, tiles_per_group)

    def prepare_and_conv(i, carry):
        prepare(i)
        conv_tiles((i + 1) * tiles_per_group, tiles_per_group)
        return carry

    lax.fori_loop(0, n_groups - 1, prepare_and_conv, 0)
    prepare(n_groups - 1)

    def scan(i, carry):
        for d in range(N_DIRS):
            c = i if d == 0 else n_chunks - 1 - i
            rows = pl.ds(pl.multiple_of(c * CHUNK, CHUNK), CHUNK)
            state = st_s[d]
            r = jnp.dot(mq_s[d, c], state.astype(BF16), preferred_element_type=F32)
            gt = jnp.broadcast_to(gt_s[d, c][0:1, :], (HEAD_DIM, HEAD_DIM))
            st_s[d] = state * gt + r[:CHUNK] + n_s[d, rows, :]
            o_s[d][rows, :] = o_s[d][rows, :] + r[CHUNK:]
        return carry

    lax.fori_loop(0, n_chunks, scan, 0, unroll=2)
    sfin_ref[0, 0] = st_s[...]

    if with_out:
        gain = gain_ref[...]

        def emit(r, carry):
            rows = pl.ds(pl.multiple_of(r * tile, tile), tile)
            o = of_s[rows, :] + ob_s[rows, :]
            o = o * lax.rsqrt(jnp.mean(o * o, axis=-1, keepdims=True) + RMS_EPS) * gain
            og_ref[0, rows, :] = (o * _silu(zp_ref[0, rows, :])).astype(BF16)
            return carry

        lax.fori_loop(0, t // tile, emit, 0)


def _gdn_scan(p, cp, rp, conv_w, o_gain, s0, with_out):
    b, t, _ = p.shape
    hd = HEAD_DIM
    col_spec = lambda off: pl.BlockSpec((1, t, hd), lambda i, h, off=off: (i, 0, off + h))
    cw_spec = lambda off: pl.BlockSpec((CONV_W, hd), lambda i, h, off=off: (0, off + h))
    state_spec = pl.BlockSpec((1, 1, N_DIRS, hd, hd), lambda i, h: (i, h, 0, 0, 0))
    out_shape = [jax.ShapeDtypeStruct((b, HEADS, N_DIRS, hd, hd), F32)]
    out_specs = [state_spec]
    if with_out:
        out_shape = [jax.ShapeDtypeStruct((b, t, HEADS * hd), BF16)] + out_shape
        out_specs = [pl.BlockSpec((1, t, hd), lambda i, h: (i, 0, h))] + out_specs
    halo = SUBLANES
    res = pl.pallas_call(
        functools.partial(_gdn_kernel, t=t, with_out=with_out),
        out_shape=tuple(out_shape),
        grid=(b, HEADS),
        in_specs=[
            col_spec(0), col_spec(HEADS), col_spec(2 * HEADS), col_spec(3 * HEADS),
            pl.BlockSpec((1, t, LANES), lambda i, h: (i, 0, 0)),
            pl.BlockSpec((1, 1, t // CHUNK, SUBLANES, CHUNK), lambda i, h: (i, h, 0, 0, 0)),
            cw_spec(0), cw_spec(HEADS), cw_spec(2 * HEADS),
            pl.BlockSpec((1, hd), lambda i, h: (0, 0)),
            state_spec,
        ],
        out_specs=tuple(out_specs),
        scratch_shapes=[
            pltpu.VMEM((3, t + 2 * halo, hd), F32),
            pltpu.VMEM((t, hd), F32), pltpu.VMEM((t, hd), F32), pltpu.VMEM((t, hd), F32),
            pltpu.VMEM((t, hd), F32), pltpu.VMEM((t, hd), F32),
            pltpu.VMEM((N_DIRS, hd, hd), F32),
            pltpu.VMEM((N_DIRS, t // CHUNK, 2 * CHUNK, hd), BF16),
            pltpu.VMEM((N_DIRS, t, hd), F32),
            pltpu.VMEM((N_DIRS, t // CHUNK, SUBLANES, hd), F32),
        ],
        compiler_params=_cparams(("parallel", "arbitrary")),
        name="gdn_scan_out" if with_out else "gdn_scan_ctx",
    )(p, p, p, p, cp, rp, conv_w, conv_w, conv_w, o_gain.reshape(1, hd), s0)
    return res


def _row_gates(cp):
    b, t, _ = cp.shape
    g = cp[:, :, :N_DIRS * GATE_LANES].reshape(b, t // CHUNK, CHUNK, N_DIRS, GATE_LANES)[..., :HEADS]
    g = g.transpose(0, 4, 1, 3, 2)
    return jnp.pad(g, ((0, 0), (0, 0), (0, 0), (0, SUBLANES - N_DIRS), (0, 0)))


def _ffn_prep_kernel(*refs, tm, with_proj):
    if with_proj:
        (og_ref, wo_ref, x_ref, gt_ref, g_ref, sh_ref, sc_ref, wrt_ref,
         h_ref, uf_ref, lgt_ref) = refs
    else:
        (x_ref, g_ref, sh_ref, sc_ref, wrt_ref, uf_ref, lgt_ref) = refs
    gain = g_ref[...]
    shift = sh_ref[0]
    scale = sc_ref[0]
    sub = 256

    for r in range(tm // sub):
        rows = slice(r * sub, (r + 1) * sub)
        h = x_ref[0, rows, :]
        if with_proj:
            h = h + gt_ref[0] * jnp.dot(og_ref[0, rows, :], wo_ref[...], preferred_element_type=F32)
            h_ref[0, rows, :] = h
        u = _norm_mod(h, gain, shift, scale)
        uf_ref[0, rows, :] = u
        lgt_ref[0, :, rows] = _mmp(wrt_ref[...], u, 2, 2, _NT)


def _ffn_prep(x, gain, shift, scale, w_router, proj=None):
    b, t, d = x.shape
    e = w_router.shape[1]
    tm = 512
    wrt = w_router.T
    row_spec = pl.BlockSpec((1, tm, d), lambda i, m: (i, m, 0))
    vec_spec = pl.BlockSpec((1, 1, d), lambda i, m: (i, 0, 0))
    common_in = [pl.BlockSpec((1, d), lambda i, m: (0, 0)), vec_spec, vec_spec,
                 pl.BlockSpec((e, d), lambda i, m: (0, 0))]
    common_args = [gain.reshape(1, d), shift, scale, wrt]
    outs = [jax.ShapeDtypeStruct((b, t, d), F32), jax.ShapeDtypeStruct((b, e, t), F32)]
    out_specs = [row_spec, pl.BlockSpec((1, e, tm), lambda i, m: (i, 0, m))]
    if proj is not None:
        og, w_out, gate = proj
        in_specs = [row_spec, pl.BlockSpec((d, d), lambda i, m: (0, 0)), row_spec, vec_spec] + common_in
        args = [og, w_out, x, gate] + common_args
        outs = [jax.ShapeDtypeStruct((b, t, d), F32)] + outs
        out_specs = [row_spec] + out_specs
    else:
        in_specs = [row_spec] + common_in
        args = [x] + common_args
    return pl.pallas_call(
        functools.partial(_ffn_prep_kernel, tm=tm, with_proj=proj is not None),
        out_shape=tuple(outs), grid=(b, t // tm), in_specs=in_specs, out_specs=tuple(out_specs),
        compiler_params=_cparams(("parallel", "parallel")),
        name="ffn_prep_proj" if proj is not None else "ffn_prep",
    )(*args)


def _residual_kernel(h_ref, m_ref, gt_ref, g_ref, *refs, tm, mode):
    sub = 256
    gain = g_ref[...]

    def body(r, carry):
        rows = pl.ds(pl.multiple_of(r * sub, sub), sub)
        h = h_ref[0, rows, :] + gt_ref[0] * m_ref[0, rows, :]
        if mode == "mix":
            sh_ref, sc_ref, ho_ref, u_ref = refs
            ho_ref[0, rows, :] = h
            u_ref[0, rows, :] = _norm_mod(h, gain, sh_ref[0], sc_ref[0])
        else:
            (o_ref,) = refs
            ms = jnp.mean(h * h, axis=-1, keepdims=True)
            o_ref[0, rows, :] = h * lax.rsqrt(ms + RMS_EPS) * gain
        return carry

    lax.fori_loop(0, tm // sub, body, 0)


def _residual(h, m, gate, gain, shift=None, scale=None):
    b, t, d = h.shape
    tm = 512
    row_spec = pl.BlockSpec((1, tm, d), lambda i, j: (i, j, 0))
    vec_spec = pl.BlockSpec((1, 1, d), lambda i, j: (i, 0, 0))
    in_specs = [row_spec, row_spec, vec_spec, pl.BlockSpec((1, d), lambda i, j: (0, 0))]
    args = [h, m, gate, gain.reshape(1, d)]
    if shift is not None:
        mode = "mix"
        in_specs += [vec_spec, vec_spec]
        args += [shift, scale]
        outs = (jax.ShapeDtypeStruct((b, t, d), F32), jax.ShapeDtypeStruct((b, t, d), F32))
        out_specs = (row_spec, row_spec)
    else:
        mode = "final"
        outs = jax.ShapeDtypeStruct((b, t, d), F32)
        out_specs = row_spec
    return pl.pallas_call(
        functools.partial(_residual_kernel, tm=tm, mode=mode),
        out_shape=outs, grid=(b, t // tm), in_specs=in_specs, out_specs=out_specs,
        compiler_params=_cparams(("parallel", "parallel")),
        name="residual_" + mode,
    )(*args)


def _pool_kernel(u_ref, h_ref, gt_ref, w_ref, sc_ref, o_ref, m1_s, *, t):
    group = pl.program_id(1)
    tile = 256
    rows_per_tile = tile // GRID_W
    n_rows = t // GRID_W
    halo = max(POOL_WINDOWS) // 2 * GRID_W
    ti = lax.broadcasted_iota(I32, (tile, tile), 0)
    si = lax.broadcasted_iota(I32, (tile, tile), 1)
    tcol = lax.broadcasted_iota(I32, (tile, 1), 0)

    def run(win):
        lo = win // 2
        hi = win - lo
        off = si - ti
        band = jnp.where((ti // GRID_W == si // GRID_W) & (off >= -lo) & (off < hi), 1.0, 0.0).astype(BF16)
        cpos = tcol % GRID_W
        inv_c = 1.0 / (jnp.minimum(cpos + hi, GRID_W) - jnp.maximum(cpos - lo, 0)).astype(F32)
        m1_s[0:halo, :] = jnp.zeros((halo, m1_s.shape[1]), F32)
        m1_s[halo + t:2 * halo + t, :] = jnp.zeros((halo, m1_s.shape[1]), F32)

        def along_w(r, carry):
            rows = pl.multiple_of(r * tile, tile)
            x = u_ref[0, pl.ds(rows, tile), :]
            tot = sum(_dg(band, p, _NN) for p in _pieces(x, 3))
            m1_s[pl.ds(rows + halo, tile), :] = tot * inv_c
            return carry

        lax.fori_loop(0, t // tile, along_w, 0)
        w = w_ref[0]
        scale = sc_ref[...]
        gate = gt_ref[0]

        def along_h(r, carry):
            rows = pl.multiple_of(r * tile, tile)
            acc = None
            for o in range(-lo, hi):
                term = m1_s[pl.ds(rows + halo + o * GRID_W, tile), :]
                acc = term if acc is None else acc + term
            rpos = r * rows_per_tile + tcol // GRID_W
            inv_r = 1.0 / (jnp.minimum(rpos + hi, n_rows) - jnp.maximum(rpos - lo, 0)).astype(F32)
            x = u_ref[0, pl.ds(rows, tile), :]
            y = _mmp(acc * inv_r - x, w, 2, 2) * scale
            o_ref[0, pl.ds(rows, tile), :] = h_ref[0, pl.ds(rows, tile), :] + gate * y
            return carry

        lax.fori_loop(0, t // tile, along_h, 0)

    for gi, win in enumerate(POOL_WINDOWS):
        pl.when(group == gi)(functools.partial(run, win))


def _pool_mixer(u, h, gate, w_grp, scale):
    b, t, d = u.shape
    groups = len(POOL_WINDOWS)
    gd = d // groups
    halo = max(POOL_WINDOWS) // 2 * GRID_W
    blk = pl.BlockSpec((1, t, gd), lambda i, g: (i, 0, g))
    return pl.pallas_call(
        functools.partial(_pool_kernel, t=t),
        out_shape=jax.ShapeDtypeStruct((b, t, d), F32),
        grid=(b, groups),
        in_specs=[blk, blk, pl.BlockSpec((1, 1, gd), lambda i, g: (i, 0, g)),
                  pl.BlockSpec((1, gd, gd), lambda i, g: (g, 0, 0)),
                  pl.BlockSpec((1, gd), lambda i, g: (0, g))],
        out_specs=blk,
        scratch_shapes=[pltpu.VMEM((t + 2 * halo, gd), F32)],
        compiler_params=_cparams(("parallel", "parallel")),
        name="pool_mixer",
    )(u, h, gate, w_grp, scale.reshape(1, d))


def _exclusive_count(mask, tri, ones):
    n = mask.shape[1] // LANES
    run = jnp.zeros((mask.shape[0], LANES), F32)
    maskf = jnp.where(mask, 1.0, 0.0)
    tiles = []
    for j in range(n):
        m = maskf[:, j * LANES:(j + 1) * LANES].astype(BF16)
        tiles.append(_dg(m, tri, _NN) + run)
        run = run + _dg(m, ones, _NN)
    return jnp.concatenate(tiles, axis=1), run


def _route_kernel(lgt_ref, o_ref, slot_s, tab_s, *, t, cap):
    e = lgt_ref.shape[1]
    lg = lgt_ref[0]
    ex = jnp.exp(lg - jnp.max(lg, axis=0, keepdims=True))
    aff = ex / jnp.sum(ex, axis=0, keepdims=True)

    def count_ge(thr):
        return jnp.sum(jnp.where(aff >= thr, 1.0, 0.0), axis=1, keepdims=True)

    n_exp = 126.0

    def pow2(k):
        return jnp.where(k >= n_exp, 0.0, jnp.exp2(-k))

    def search_exponent(i, carry):
        k_hi, k_lo = carry
        k_mid = jnp.floor(0.5 * (k_hi + k_lo))
        ok = count_ge(pow2(k_mid)) >= cap
        return jnp.where(ok, k_hi, k_mid), jnp.where(ok, k_mid, k_lo)

    k_hi, k_lo = lax.fori_loop(0, 8, search_exponent,
                               (jnp.full((e, 1), -1.0, F32), jnp.full((e, 1), n_exp, F32)))

    def search_value(i, carry):
        lo, hi = carry
        mid = 0.5 * (lo + hi)
        ok = count_ge(mid) >= cap
        return jnp.where(ok, mid, lo), jnp.where(ok, hi, mid)

    lo, hi = lax.fori_loop(0, 26, search_value, (pow2(k_lo), pow2(k_hi)))
    above = aff >= hi
    equal = (aff >= lo) & (aff < hi)
    ri = lax.broadcasted_iota(I32, (LANES, LANES), 0)
    ci = lax.broadcasted_iota(I32, (LANES, LANES), 1)
    tri = jnp.where(ri < ci, 1.0, 0.0).astype(BF16)
    ones = jnp.ones((LANES, LANES), BF16)
    n_above = jnp.sum(jnp.where(above, 1.0, 0.0), axis=1, keepdims=True)
    eq_rank, _ = _exclusive_count(equal, tri, ones)
    sel = above | (equal & (eq_rank < cap - n_above))
    slot, _ = _exclusive_count(sel, tri, ones)
    slot = jnp.where(sel, slot, -1.0)
    for x in range(e):
        slot_s[x] = slot[x:x + 1, :]

    tok = lax.broadcasted_iota(I32, (SUBLANES, t), 1)
    sub = lax.broadcasted_iota(I32, (SUBLANES, t), 0)
    head_rows = jnp.where(sub == 0, (tok // INDEX_BASE).astype(F32),
                          jnp.where(sub == 1, (tok % INDEX_BASE).astype(F32), 0.0))
    pad_rows = jnp.zeros((ROUTE_ROWS - SUBLANES - 3 * e, t), F32)
    tab_s[...] = jnp.concatenate([head_rows] + [p.astype(F32) for p in _pieces(aff, 3)] + [pad_rows],
                                 axis=0).astype(BF16)

    s_iota = lax.broadcasted_iota(I32, (cap, t), 0).astype(F32)

    def compact(x, carry):
        onehot = jnp.where(s_iota == slot_s[x], 1.0, 0.0).astype(BF16)
        o_ref[0, x] = _dg(tab_s[...], onehot, _NT)
        return carry

    lax.fori_loop(0, e, compact, 0)


def _route(lgt, cap):
    b, e, t = lgt.shape
    assert SUBLANES + 3 * e <= ROUTE_ROWS
    packed = pl.pallas_call(
        functools.partial(_route_kernel, t=t, cap=cap),
        out_shape=jax.ShapeDtypeStruct((b, e, ROUTE_ROWS, cap), F32),
        grid=(b,),
        in_specs=[pl.BlockSpec((1, e, t), lambda i: (i, 0, 0))],
        out_specs=pl.BlockSpec((1, e, ROUTE_ROWS, cap), lambda i: (i, 0, 0, 0)),
        scratch_shapes=[pltpu.VMEM((e, 1, t), F32), pltpu.VMEM((ROUTE_ROWS, t), BF16)],
        compiler_params=_cparams(("parallel",)),
        name="route_topk",
    )(lgt)
    idx = (packed[:, :, 0, :] * INDEX_BASE + packed[:, :, 1, :]).astype(I32)
    pieces = packed[:, :, SUBLANES:SUBLANES + 3 * e, :].reshape(b, e, 3, e, cap).sum(axis=2)
    gates = jnp.take_along_axis(pieces, jnp.arange(e).reshape(1, e, 1, 1), axis=2)[:, :, 0, :]
    return idx, gates


def _ffn_kernel(idx_ref, idx_next_ref, uf_hbm, wg_ref, wu_ref, wd_ref, ys_ref,
                x_s, xb_s, wg_s, wu_s, wd_s, sem, *, rows, tm):
    x, f = pl.program_id(0), pl.program_id(1)
    n_x, n_f = pl.num_programs(0), pl.num_programs(1)
    slot = x % 2
    n_tiles = rows // tm
    per_tile = rows // (n_f * n_tiles)

    def row_copy(token, r, s):
        return pltpu.make_async_copy(uf_hbm.at[pl.ds(token, 1), :], x_s.at[s, pl.ds(r, 1), :], sem.at[s])

    def all_rows(s):
        return pltpu.make_async_copy(uf_hbm.at[pl.ds(0, rows), :], x_s.at[s], sem.at[s])

    @pl.when((x == 0) & (f == 0))
    def _():
        def issue(r, carry):
            row_copy(idx_ref[0, 0, r], r, 0).start()
            return carry

        lax.fori_loop(0, rows, issue, 0, unroll=8)

    @pl.when(f == 0)
    def _():
        all_rows(slot).wait()

        def cast(r, carry):
            rr = pl.ds(pl.multiple_of(r * tm, tm), tm)
            xb_s[rr, :] = x_s[slot, rr, :].astype(BF16)
            return carry

        lax.fori_loop(0, n_tiles, cast, 0)

    wg_s[...] = wg_ref[0, 0].astype(BF16)
    wu_s[...] = wu_ref[0, 0].astype(BF16)
    wd_s[...] = wd_ref[0, 0].astype(BF16)

    def body(m, carry):
        base = (f * n_tiles + m) * per_tile
        for j in range(per_tile):
            row_copy(idx_next_ref[0, 0, base + j], base + j, 1 - slot).start()
        rr = pl.ds(pl.multiple_of(m * tm, tm), tm)
        xt = xb_s[rr, :]
        g = jnp.dot(xt, wg_s[...], preferred_element_type=F32)
        u = jnp.dot(xt, wu_s[...], preferred_element_type=F32)
        hid = (_silu(g) * u).astype(BF16)
        y = jnp.dot(hid, wd_s[...], preferred_element_type=F32)

        @pl.when(f == 0)
        def _():
            ys_ref[0, rr, :] = y

        @pl.when(f != 0)
        def _():
            ys_ref[0, rr, :] = ys_ref[0, rr, :] + y

        return carry

    lax.fori_loop(0, n_tiles, body, 0)

    @pl.when((x == n_x - 1) & (f == n_f - 1))
    def _():
        all_rows(1 - slot).wait()


def _expert_ffn(uf_flat, gidx, w_gate, w_up, w_down, layer):
    _, e, d, de = w_gate.shape
    rows = gidx.shape[-1]
    tf = 512
    tm = 512
    assert rows % tm == 0 and rows % ((rows // tm) * (de // tf)) == 0
    return pl.pallas_call(
        functools.partial(_ffn_kernel, rows=rows, tm=tm),
        out_shape=jax.ShapeDtypeStruct((e, rows, d), F32),
        grid=(e, de // tf),
        in_specs=[
            pl.BlockSpec((1, 1, rows), lambda x, f: (x, 0, 0), memory_space=pltpu.SMEM),
            pl.BlockSpec((1, 1, rows), lambda x, f: (jnp.minimum(x + 1, e - 1), 0, 0), memory_space=pltpu.SMEM),
            pl.BlockSpec(memory_space=pl.ANY),
            pl.BlockSpec((1, 1, d, tf), lambda x, f: (layer, x, 0, f)),
            pl.BlockSpec((1, 1, d, tf), lambda x, f: (layer, x, 0, f)),
            pl.BlockSpec((1, 1, tf, d), lambda x, f: (layer, x, f, 0)),
        ],
        out_specs=pl.BlockSpec((1, rows, d), lambda x, f: (x, 0, 0)),
        scratch_shapes=[pltpu.VMEM((2, rows, d), F32), pltpu.VMEM((rows, d), BF16),
                        pltpu.VMEM((d, tf), BF16), pltpu.VMEM((d, tf), BF16), pltpu.VMEM((tf, d), BF16),
                        pltpu.SemaphoreType.DMA((2,))],
        compiler_params=_cparams(("arbitrary", "arbitrary")),
        name="expert_ffn",
    )(gidx, gidx, uf_flat, w_gate, w_up, w_down)


def _combine_kernel(idx_ref, gate_ref, ys_ref, o_ref, *, cap, t):
    @pl.when(pl.program_id(1) == 0)
    def _():
        tile = 256

        def clear(r, carry):
            o_ref[0, pl.ds(pl.multiple_of(r * tile, tile), tile), :] = jnp.zeros((tile, o_ref.shape[2]), F32)
            return carry

        lax.fori_loop(0, t // tile, clear, 0)

    rows_per_group = SUBLANES

    def body(g, carry):
        base = pl.multiple_of(g * rows_per_group, rows_per_group)
        ys = ys_ref[0, pl.ds(base, rows_per_group), :]
        toks = [idx_ref[0, 0, base + k] for k in range(rows_per_group)]
        new = [o_ref[0, pl.ds(toks[k], 1), :] + gate_ref[0, 0, base + k] * ys[k:k + 1, :]
               for k in range(rows_per_group)]
        for k in range(rows_per_group):
            o_ref[0, pl.ds(toks[k], 1), :] = new[k]
        return carry

    lax.fori_loop(0, cap // rows_per_group, body, 0)


def _combine(ys, idx, gates, b, t):
    e, _, d = ys.shape
    cap = idx.shape[-1]
    smem = lambda: pl.BlockSpec((1, 1, cap), lambda i, x: (i * e + x, 0, 0), memory_space=pltpu.SMEM)
    return pl.pallas_call(
        functools.partial(_combine_kernel, cap=cap, t=t),
        out_shape=jax.ShapeDtypeStruct((b, t, d), F32),
        grid=(b, e),
        in_specs=[smem(), smem(), pl.BlockSpec((1, cap, d), lambda i, x: (x, i, 0))],
        out_specs=pl.BlockSpec((1, t, d), lambda i, x: (i, 0, 0)),
        compiler_params=_cparams(("parallel", "arbitrary")),
        name="moe_combine",
    )(idx.reshape(b * e, 1, cap), gates.reshape(b * e, 1, cap), ys)


def _ec_moe(uf, lgt, w_gate, w_up, w_down, layer):
    b, t, d = uf.shape
    e = w_gate.shape[1]
    cap = (EC_CAPACITY_FACTOR * t) // e
    idx, gates = _route(lgt, cap)
    gidx = (idx + (jnp.arange(b, dtype=I32) * t)[:, None, None]).transpose(1, 0, 2).reshape(e, 1, b * cap)
    ys = _expert_ffn(uf.reshape(b * t, d), gidx, w_gate, w_up, w_down, layer)
    return _combine(ys, idx, gates, b, t)


def kernel(x, c, ctx, c_ctx, w_mod, b_mod, norm_mix_g, norm_ffn_g, gdn_w_in, gdn_conv_w, gdn_a_log, gdn_dt_bias, gdn_o_gain, gdn_w_out, pool_w, pool_scale, moe_w_router, moe_w_gate, moe_w_up, moe_w_down, final_g):
    b, t, d = x.shape
    assert b < MOD_ROWS and t % (2 * CHUNK) == 0 and ctx.shape[1] % (2 * CHUNK) == 0
    assert d == HEADS * HEAD_DIM and w_mod.shape[0] == 2

    cc = jnp.zeros((MOD_ROWS, d), F32).at[:b].set(c).at[b].set(c_ctx)
    mod = _modulation(cc, w_mod, b_mod)

    def mod_rows(layer, row_slice, bcast):
        parts = []
        for i in range(N_MOD):
            m = mod[layer, row_slice, i * d:(i + 1) * d]
            parts.append(jnp.broadcast_to(m, (b, d)).reshape(b, 1, d) if bcast else m.reshape(b, 1, d))
        return parts

    sh_m, sc_m, gt_m, sh_f, sc_f, gt_f = mod_rows(0, slice(0, b), False)
    csh_m, csc_m = mod_rows(0, slice(b, b + 1), True)[:2]

    w_in = gdn_w_in[0]
    n_main = 4 * HEADS * HEAD_DIM
    w_main = w_in[:, :n_main].astype(BF16)
    w_ab = jnp.pad(w_in[:, n_main:], ((0, 0), (0, LANES - (w_in.shape[1] - n_main))))

    def gate_lanes(v):
        v = jnp.pad(v, ((0, 0), (0, GATE_LANES - HEADS))).reshape(1, N_DIRS * GATE_LANES)
        return jnp.pad(v, ((0, 0), (0, LANES - N_DIRS * GATE_LANES)))

    alog_l = gate_lanes(gdn_a_log[0])
    dtb_l = gate_lanes(gdn_dt_bias[0])
    p_lat, cp_lat = _gdn_project(x, norm_mix_g[0], sh_m, sc_m, w_main, w_ab, alog_l, dtb_l, tm=1024)
    p_ctx, cp_ctx = _gdn_project(ctx, norm_mix_g[0], csh_m, csc_m, w_main, w_ab, alog_l, dtb_l, tm=ctx.shape[1])
    s0 = jnp.zeros((b, HEADS, N_DIRS, HEAD_DIM, HEAD_DIM), F32)
    (s_ctx,) = _gdn_scan(p_ctx, cp_ctx, _row_gates(cp_ctx), gdn_conv_w[0], gdn_o_gain[0], s0, False)
    og, _ = _gdn_scan(p_lat, cp_lat, _row_gates(cp_lat), gdn_conv_w[0], gdn_o_gain[0], s_ctx, True)
    h, uf, lgt = _ffn_prep(x, norm_ffn_g[0], sh_f, sc_f, moe_w_router[0],
                                proj=(og, gdn_w_out[0].astype(BF16), gt_m))
    moe = _ec_moe(uf, lgt, moe_w_gate, moe_w_up, moe_w_down, 0)

    sh_m, sc_m, gt_m, sh_f1, sc_f1, gt_f1 = mod_rows(1, slice(0, b), False)
    h, u = _residual(h, moe, gt_f, norm_mix_g[1], sh_m, sc_m)
    h = _pool_mixer(u, h, gt_m, pool_w[0], pool_scale[0])
    uf, lgt = _ffn_prep(h, norm_ffn_g[1], sh_f1, sc_f1, moe_w_router[1])
    moe = _ec_moe(uf, lgt, moe_w_gate, moe_w_up, moe_w_down, 1)
    return _residual(h, moe, gt_f1, final_g)
```

```python
import functools

import jax
import jax.numpy as jnp
from jax import lax
from jax.experimental import pallas as pl
from jax.experimental.pallas import tpu as pltpu

F32, BF16, I32 = jnp.float32, jnp.bfloat16, jnp.int32

GRID_W = 64
N_DIRS = 2
HEADS = 8
HEAD_DIM = 128
CONV_W = 5
POOL_WINDOWS = (2, 4, 8, 16)
N_EXPERTS = 16
EC_CAPACITY_FACTOR = 2
N_MOD = 6
RMS_EPS = 1e-6

LANES = 128
SUBLANES = 8
VMEM_LIMIT_BYTES = 58 * 1024 * 1024

CHUNK = 128
MOD_ROWS = 8
GATE_LANES = 16
ROUTE_ROWS = 64
INDEX_BASE = 64


def _sigmoid(x):
    return 1.0 / (1.0 + jnp.exp(-x))


def _silu(x):
    return x * _sigmoid(x)


def _softplus(x):
    return jnp.maximum(x, 0.0) + jnp.log(1.0 + jnp.exp(-jnp.abs(x)))


def _pieces(x, n):
    out = []
    r = x
    for i in range(n):
        p = r.astype(BF16)
        out.append(p)
        if i + 1 < n:
            r = r - p.astype(F32)
    return out


_NN = (((1,), (0,)), ((), ()))
_NT = (((1,), (1,)), ((), ()))
_TN = (((0,), (0,)), ((), ()))


def _dg(a, b, dims):
    return lax.dot_general(a, b, dims, preferred_element_type=F32)


def _mm(a, b, dims=_NN):
    return _dg(a.astype(BF16), b.astype(BF16), dims)


def _mmp(a, b, na, nb, dims=_NN):
    pa = _pieces(a, na) if na > 1 else [a.astype(BF16)]
    pb = _pieces(b, nb) if nb > 1 else [b.astype(BF16)]
    order = max(na, nb)
    acc = None
    for i, x in enumerate(pa):
        for j, y in enumerate(pb):
            if i + j < order:
                t = _dg(x, y, dims)
                acc = t if acc is None else acc + t
    return acc


def _norm_mod(x, gain, shift, scale):
    ms = jnp.mean(x * x, axis=-1, keepdims=True)
    y = x * lax.rsqrt(ms + RMS_EPS) * gain
    return y * (1.0 + scale) + shift


def _cparams(sem):
    return pltpu.CompilerParams(dimension_semantics=sem, vmem_limit_bytes=VMEM_LIMIT_BYTES)


def _mod_kernel(cc_ref, w_ref, b_ref, o_ref):
    act = _silu(cc_ref[...])
    o_ref[0] = _mmp(act, w_ref[0], 2, 2) + b_ref[0]


def _modulation(cc, w_mod, b_mod):
    depth, d, nd = w_mod.shape
    return pl.pallas_call(
        _mod_kernel,
        out_shape=jax.ShapeDtypeStruct((depth, MOD_ROWS, nd), F32),
        grid=(depth, nd // d),
        in_specs=[
            pl.BlockSpec((MOD_ROWS, d), lambda l, j: (0, 0)),
            pl.BlockSpec((1, d, d), lambda l, j: (l, 0, j)),
            pl.BlockSpec((1, 1, d), lambda l, j: (l, 0, j)),
        ],
        out_specs=pl.BlockSpec((1, MOD_ROWS, d), lambda l, j: (l, 0, j)),
        compiler_params=_cparams(("parallel", "parallel")),
        name="modulation",
    )(cc, w_mod, b_mod.reshape(depth, 1, nd))


def _proj_kernel(x_ref, g_ref, sh_ref, sc_ref, w_ref, wab_ref, alog_ref, dtb_ref,
                 p_ref, cp_ref, u_s, *, tm):
    @pl.when(pl.program_id(2) == 0)
    def _():
        gain = g_ref[...]
        shift = sh_ref[0]
        scale = sc_ref[0]
        lane = lax.broadcasted_iota(I32, (CHUNK, LANES), 1)
        row = lax.broadcasted_iota(I32, (CHUNK, CHUNK), 0)
        col = lax.broadcasted_iota(I32, (CHUNK, CHUNK), 1)
        summing = jnp.concatenate([jnp.where(col <= row, 1.0, 0.0), jnp.where(col >= row, 1.0, 0.0),
                                   jnp.ones((CHUNK, CHUNK), F32)], axis=0).astype(BF16)
        is_decay = (lane % GATE_LANES < HEADS) & (lane < N_DIRS * GATE_LANES)
        n_chunks = tm // CHUNK
        group = 4 if n_chunks % 4 == 0 else 2
        wab = wab_ref[...]

        def body(r, carry):
            rows = [pl.ds(pl.multiple_of((r * group + k) * CHUNK, CHUNK), CHUNK) for k in range(group)]
            us = [_norm_mod(x_ref[0, rr, :], gain, shift, scale) for rr in rows]
            for rr, u in zip(rows, us):
                u_s[rr, :] = u.astype(BF16)
            abs_ = [_mmp(u, wab, 2, 2) for u in us]
            gs = [jnp.where(is_decay, -jnp.exp(alog_ref[...]) * _softplus(ab + dtb_ref[...]), 0.0) for ab in abs_]
            sums = [sum(_dg(summing, x, _NN) for x in _pieces(g, 3)) for g in gs]
            for rr, ab, s in zip(rows, abs_, sums):
                gc = jnp.where(lane < GATE_LANES, s[:CHUNK], s[CHUNK:2 * CHUNK])
                pk = jnp.where(is_decay, gc, _sigmoid(ab))
                pk = jnp.where(lane < N_DIRS * GATE_LANES, pk,
                               pltpu.roll(s[2 * CHUNK:], N_DIRS * GATE_LANES, axis=1))
                cp_ref[0, rr, :] = jnp.where(lane < 2 * N_DIRS * GATE_LANES, pk, 0.0)
            return carry

        lax.fori_loop(0, n_chunks // group, body, 0)

    p_ref[0] = jnp.dot(u_s[...], w_ref[...], preferred_element_type=F32)


def _gdn_project(x, gain, shift, scale, w_main, w_ab, alog_l, dtb_l, tm):
    b, t, d = x.shape
    n_main = w_main.shape[1]
    tn = 1024
    return pl.pallas_call(
        functools.partial(_proj_kernel, tm=tm),
        out_shape=(jax.ShapeDtypeStruct((b, t, n_main), F32),
                   jax.ShapeDtypeStruct((b, t, LANES), F32)),
        grid=(b, t // tm, n_main // tn),
        in_specs=[
            pl.BlockSpec((1, tm, d), lambda i, m, j: (i, m, 0)),
            pl.BlockSpec((1, d), lambda i, m, j: (0, 0)),
            pl.BlockSpec((1, 1, d), lambda i, m, j: (i, 0, 0)),
            pl.BlockSpec((1, 1, d), lambda i, m, j: (i, 0, 0)),
            pl.BlockSpec((d, tn), lambda i, m, j: (0, j)),
            pl.BlockSpec((d, LANES), lambda i, m, j: (0, 0)),
            pl.BlockSpec((1, LANES), lambda i, m, j: (0, 0)),
            pl.BlockSpec((1, LANES), lambda i, m, j: (0, 0)),
        ],
        out_specs=(pl.BlockSpec((1, tm, tn), lambda i, m, j: (i, m, j)),
                   pl.BlockSpec((1, tm, LANES), lambda i, m, j: (i, m, 0))),
        scratch_shapes=[pltpu.VMEM((tm, d), BF16)],
        compiler_params=_cparams(("parallel", "parallel", "arbitrary")),
        name="gdn_project",
    )(x, gain.reshape(1, d), shift, scale, w_main, w_ab, alog_l, dtb_l)


def _neumann_inverses(mats, eye):
    def tile(hi, lo):
        return jnp.concatenate([jnp.concatenate([hi, lo], axis=1),
                                jnp.concatenate([hi, jnp.zeros_like(hi)], axis=1)], axis=0)

    def fold(r):
        return r[:, :CHUNK] + r[:, CHUNK:]

    ps = [eye - a for a in mats]
    halves = [_pieces(a, 2) for a in mats]
    bs = [fold(_dg(jnp.concatenate(h, axis=1), tile(*h), _NN)) for h in halves]
    levels = CHUNK.bit_length() - 1
    for lvl in range(1, levels):
        b_halves = [_pieces(b, 2) for b in bs]
        p_halves = [_pieces(p, 2) for p in ps]
        wts = [tile(*h) for h in b_halves]
        if lvl + 1 < levels:
            lhs = [jnp.concatenate([jnp.concatenate(ph, axis=1), jnp.concatenate(bh, axis=1)], axis=0)
                   for ph, bh in zip(p_halves, b_halves)]
            pbs = [fold(_dg(x, wt, _NN)) for x, wt in zip(lhs, wts)]
            ps = [p + pb[:CHUNK] for p, pb in zip(ps, pbs)]
            bs = [pb[CHUNK:] for pb in pbs]
        else:
            ps = [p + fold(_dg(jnp.concatenate(ph, axis=1), wt, _NN)) for p, ph, wt in zip(ps, p_halves, wts)]
    return ps


def _gdn_kernel(qp_ref, kp_ref, vp_ref, zp_ref, cp_ref, rp_ref, cwq_ref, cwk_ref, cwv_ref,
                gain_ref, s0_ref, *rest, t, with_out):
    if with_out:
        og_ref, sfin_ref = rest[:2]
        pad_s, q_s, k_s, v_s, of_s, ob_s, st_s, mq_s, n_s, gt_s = rest[2:]
    else:
        (sfin_ref,) = rest[:1]
        pad_s, q_s, k_s, v_s, of_s, ob_s, st_s, mq_s, n_s, gt_s = rest[1:]
    head = pl.program_id(1)
    n_chunks = t // CHUNK
    halo = SUBLANES

    row = lax.broadcasted_iota(I32, (CHUNK, CHUNK), 0)
    col = lax.broadcasted_iota(I32, (CHUNK, CHUNK), 1)

    tile = 256
    zeros_halo = jnp.zeros((halo, HEAD_DIM), F32)
    streams = ((qp_ref, cwq_ref, q_s, "q"), (kp_ref, cwk_ref, k_s, "k"), (vp_ref, cwv_ref, v_s, "v"))
    for n, (src_ref, _, _, _) in enumerate(streams):
        pad_s[n, 0:halo, :] = zeros_halo
        pad_s[n, t + halo:t + 2 * halo, :] = zeros_halo

        def fill(r, carry, src_ref=src_ref, n=n):
            rows = pl.multiple_of(r * tile, tile)
            pad_s[n, pl.ds(rows + halo, tile), :] = src_ref[0, pl.ds(rows, tile), :]
            return carry

        lax.fori_loop(0, t // tile, fill, 0)
    conv_w = [cw_ref[...] for _, cw_ref, _, _ in streams]

    def conv_tiles(first, count):
        for tt in range(count):
            rows = pl.multiple_of((first + tt) * tile, tile)
            for n, (_, _, dst, mode) in enumerate(streams):
                win = pad_s[n, pl.ds(rows, tile + 2 * halo), :]
                acc = None
                for j in range(CONV_W):
                    off = halo + j - CONV_W // 2
                    term = conv_w[n][j:j + 1, :] * win[off:off + tile, :]
                    acc = term if acc is None else acc + term
                y = _silu(acc)
                if mode != "v":
                    y = y * lax.rsqrt(jnp.sum(y * y, axis=-1, keepdims=True) + RMS_EPS)
                if mode == "q":
                    y = y * (HEAD_DIM ** -0.5)
                dst[pl.ds(rows, tile), :] = y

    st_s[...] = s0_ref[0, 0]
    eye = jnp.where(row == col, 1.0, 0.0)
    masks = ((col <= row, col < row), (col >= row, col > row))
    o_s = (of_s, ob_s)

    group = 4 if n_chunks % 4 == 0 else 2

    def prepare(i):
        chains = []
        for g in range(group):
            c = i * group + g
            rows = pl.ds(pl.multiple_of(c * CHUNK, CHUNK), CHUNK)
            q = q_s[rows, :]
            k = k_s[rows, :]
            v = v_s[rows, :]
            cp = pltpu.roll(cp_ref[0, rows, :], LANES - head, axis=1)
            rp = rp_ref[0, 0, c]
            kq = _mm(jnp.concatenate([k, q], axis=0), k, _NT)
            for d in range(N_DIRS):
                lane = d * GATE_LANES
                bcast = lambda j: jnp.broadcast_to(cp[:, j:j + 1], (CHUNK, HEAD_DIM))
                gc, beta, gtot = bcast(lane), bcast(lane + HEADS), bcast(lane + N_DIRS * GATE_LANES)
                grow = jnp.broadcast_to(rp[d:d + 1, :], (CHUNK, CHUNK))
                incl, strict = masks[d]
                decay = jnp.where(incl, jnp.exp(jnp.where(incl, gc - grow, 0.0)), 0.0)
                e_in = jnp.exp(gc)
                chains.append(dict(
                    d=d, c=c, rows=rows, qd=q * e_in, gtot=gtot,
                    a=jnp.where(strict, beta * kq[:CHUNK] * decay, 0.0),
                    att=kq[CHUNK:] * decay,
                    rhs=jnp.concatenate([v * beta, k * (beta * e_in)], axis=1),
                    kd=k * jnp.exp(gtot - gc)))
        tinvs = _neumann_inverses([ch["a"] for ch in chains], eye)
        uws = [_mm(tinv, ch["rhs"]) for tinv, ch in zip(tinvs, chains)]
        rs = [_mm(jnp.concatenate([ch["att"], ch["kd"].T], axis=0), uw) for ch, uw in zip(chains, uws)]
        for ch, r in zip(chains, rs):
            d, c, rows = ch["d"], ch["c"], ch["rows"]
            mq_s[d, c] = jnp.concatenate([-r[CHUNK:, HEAD_DIM:], ch["qd"] - r[:CHUNK, HEAD_DIM:]],
                                         axis=0).astype(BF16)
            n_s[d, rows, :] = r[CHUNK:, :HEAD_DIM]
            o_s[d][rows, :] = r[:CHUNK, :HEAD_DIM]
            gt_s[d, c] = jnp.exp(ch["gtot"][:SUBLANES])

    n_groups = n_chunks // group
    tiles_per_group = group * CHUNK // tile
    conv_tiles(0, tiles_per_group)

    def prepare_and_conv(i, carry):
        prepare(i)
        conv_tiles((i + 1) * tiles_per_group, tiles_per_group)
        return carry

    lax.fori_loop(0, n_groups - 1, prepare_and_conv, 0)
    prepare(n_groups - 1)

    def scan(i, carry):
        for d in range(N_DIRS):
            c = i if d == 0 else n_chunks - 1 - i
            rows = pl.ds(pl.multiple_of(c * CHUNK, CHUNK), CHUNK)
            state = st_s[d]
            r = jnp.dot(mq_s[d, c], state.astype(BF16), preferred_element_type=F32)
            gt = jnp.broadcast_to(gt_s[d, c][0:1, :], (HEAD_DIM, HEAD_DIM))
            st_s[d] = state * gt + r[:CHUNK] + n_s[d, rows, :]
            o_s[d][rows, :] = o_s[d][rows, :] + r[CHUNK:]
        return carry

    lax.fori_loop(0, n_chunks, scan, 0, unroll=2)
    sfin_ref[0, 0] = st_s[...]

    if with_out:
        gain = gain_ref[...]

        def emit(r, carry):
            rows = pl.ds(pl.multiple_of(r * tile, tile), tile)
            o = of_s[rows, :] + ob_s[rows, :]
            o = o * lax.rsqrt(jnp.mean(o * o, axis=-1, keepdims=True) + RMS_EPS) * gain
            og_ref[0, rows, :] = (o * _silu(zp_ref[0, rows, :])).astype(BF16)
            return carry

        lax.fori_loop(0, t // tile, emit, 0)


def _gdn_scan(p, cp, rp, conv_w, o_gain, s0, with_out):
    b, t, _ = p.shape
    hd = HEAD_DIM
    col_spec = lambda off: pl.BlockSpec((1, t, hd), lambda i, h, off=off: (i, 0, off + h))
    cw_spec = lambda off: pl.BlockSpec((CONV_W, hd), lambda i, h, off=off: (0, off + h))
    state_spec = pl.BlockSpec((1, 1, N_DIRS, hd, hd), lambda i, h: (i, h, 0, 0, 0))
    out_shape = [jax.ShapeDtypeStruct((b, HEADS, N_DIRS, hd, hd), F32)]
    out_specs = [state_spec]
    if with_out:
        out_shape = [jax.ShapeDtypeStruct((b, t, HEADS * hd), BF16)] + out_shape
        out_specs = [pl.BlockSpec((1, t, hd), lambda i, h: (i, 0, h))] + out_specs
    halo = SUBLANES
    res = pl.pallas_call(
        functools.partial(_gdn_kernel, t=t, with_out=with_out),
        out_shape=tuple(out_shape),
        grid=(b, HEADS),
        in_specs=[
            col_spec(0), col_spec(HEADS), col_spec(2 * HEADS), col_spec(3 * HEADS),
            pl.BlockSpec((1, t, LANES), lambda i, h: (i, 0, 0)),
            pl.BlockSpec((1, 1, t // CHUNK, SUBLANES, CHUNK), lambda i, h: (i, h, 0, 0, 0)),
            cw_spec(0), cw_spec(HEADS), cw_spec(2 * HEADS),
            pl.BlockSpec((1, hd), lambda i, h: (0, 0)),
            state_spec,
        ],
        out_specs=tuple(out_specs),
        scratch_shapes=[
            pltpu.VMEM((3, t + 2 * halo, hd), F32),
            pltpu.VMEM((t, hd), F32), pltpu.VMEM((t, hd), F32), pltpu.VMEM((t, hd), F32),
            pltpu.VMEM((t, hd), F32), pltpu.VMEM((t, hd), F32),
            pltpu.VMEM((N_DIRS, hd, hd), F32),
            pltpu.VMEM((N_DIRS, t // CHUNK, 2 * CHUNK, hd), BF16),
            pltpu.VMEM((N_DIRS, t, hd), F32),
            pltpu.VMEM((N_DIRS, t // CHUNK, SUBLANES, hd), F32),
        ],
        compiler_params=_cparams(("parallel", "arbitrary")),
        name="gdn_scan_out" if with_out else "gdn_scan_ctx",
    )(p, p, p, p, cp, rp, conv_w, conv_w, conv_w, o_gain.reshape(1, hd), s0)
    return res


def _row_gates(cp):
    b, t, _ = cp.shape
    g = cp[:, :, :N_DIRS * GATE_LANES].reshape(b, t // CHUNK, CHUNK, N_DIRS, GATE_LANES)[..., :HEADS]
    g = g.transpose(0, 4, 1, 3, 2)
    return jnp.pad(g, ((0, 0), (0, 0), (0, 0), (0, SUBLANES - N_DIRS), (0, 0)))


def _ffn_prep_kernel(*refs, tm, with_proj):
    if with_proj:
        (og_ref, wo_ref, x_ref, gt_ref, g_ref, sh_ref, sc_ref, wrt_ref,
         h_ref, uf_ref, lgt_ref) = refs
    else:
        (x_ref, g_ref, sh_ref, sc_ref, wrt_ref, uf_ref, lgt_ref) = refs
    gain = g_ref[...]
    shift = sh_ref[0]
    scale = sc_ref[0]
    sub = 256

    for r in range(tm // sub):
        rows = slice(r * sub, (r + 1) * sub)
        h = x_ref[0, rows, :]
        if with_proj:
            h = h + gt_ref[0] * jnp.dot(og_ref[0, rows, :], wo_ref[...], preferred_element_type=F32)
            h_ref[0, rows, :] = h
        u = _norm_mod(h, gain, shift, scale)
        uf_ref[0, rows, :] = u
        lgt_ref[0, :, rows] = _mmp(wrt_ref[...], u, 2, 2, _NT)


def _ffn_prep(x, gain, shift, scale, w_router, proj=None):
    b, t, d = x.shape
    e = w_router.shape[1]
    tm = 512
    wrt = w_router.T
    row_spec = pl.BlockSpec((1, tm, d), lambda i, m: (i, m, 0))
    vec_spec = pl.BlockSpec((1, 1, d), lambda i, m: (i, 0, 0))
    common_in = [pl.BlockSpec((1, d), lambda i, m: (0, 0)), vec_spec, vec_spec,
                 pl.BlockSpec((e, d), lambda i, m: (0, 0))]
    common_args = [gain.reshape(1, d), shift, scale, wrt]
    outs = [jax.ShapeDtypeStruct((b, t, d), F32), jax.ShapeDtypeStruct((b, e, t), F32)]
    out_specs = [row_spec, pl.BlockSpec((1, e, tm), lambda i, m: (i, 0, m))]
    if proj is not None:
        og, w_out, gate = proj
        in_specs = [row_spec, pl.BlockSpec((d, d), lambda i, m: (0, 0)), row_spec, vec_spec] + common_in
        args = [og, w_out, x, gate] + common_args
        outs = [jax.ShapeDtypeStruct((b, t, d), F32)] + outs
        out_specs = [row_spec] + out_specs
    else:
        in_specs = [row_spec] + common_in
        args = [x] + common_args
    return pl.pallas_call(
        functools.partial(_ffn_prep_kernel, tm=tm, with_proj=proj is not None),
        out_shape=tuple(outs), grid=(b, t // tm), in_specs=in_specs, out_specs=tuple(out_specs),
        compiler_params=_cparams(("parallel", "parallel")),
        name="ffn_prep_proj" if proj is not None else "ffn_prep",
    )(*args)


def _residual_kernel(h_ref, m_ref, gt_ref, g_ref, *refs, tm, mode):
    sub = 256
    gain = g_ref[...]

    def body(r, carry):
        rows = pl.ds(pl.multiple_of(r * sub, sub), sub)
        h = h_ref[0, rows, :] + gt_ref[0] * m_ref[0, rows, :]
        if mode == "mix":
            sh_ref, sc_ref, ho_ref, u_ref = refs
            ho_ref[0, rows, :] = h
            u_ref[0, rows, :] = _norm_mod(h, gain, sh_ref[0], sc_ref[0])
        else:
            (o_ref,) = refs
            ms = jnp.mean(h * h, axis=-1, keepdims=True)
            o_ref[0, rows, :] = h * lax.rsqrt(ms + RMS_EPS) * gain
        return carry

    lax.fori_loop(0, tm // sub, body, 0)


def _residual(h, m, gate, gain, shift=None, scale=None):
    b, t, d = h.shape
    tm = 512
    row_spec = pl.BlockSpec((1, tm, d), lambda i, j: (i, j, 0))
    vec_spec = pl.BlockSpec((1, 1, d), lambda i, j: (i, 0, 0))
    in_specs = [row_spec, row_spec, vec_spec, pl.BlockSpec((1, d), lambda i, j: (0, 0))]
    args = [h, m, gate, gain.reshape(1, d)]
    if shift is not None:
        mode = "mix"
        in_specs += [vec_spec, vec_spec]
        args += [shift, scale]
        outs = (jax.ShapeDtypeStruct((b, t, d), F32), jax.ShapeDtypeStruct((b, t, d), F32))
        out_specs = (row_spec, row_spec)
    else:
        mode = "final"
        outs = jax.ShapeDtypeStruct((b, t, d), F32)
        out_specs = row_spec
    return pl.pallas_call(
        functools.partial(_residual_kernel, tm=tm, mode=mode),
        out_shape=outs, grid=(b, t // tm), in_specs=in_specs, out_specs=out_specs,
        compiler_params=_cparams(("parallel", "parallel")),
        name="residual_" + mode,
    )(*args)


def _pool_kernel(u_ref, h_ref, gt_ref, w_ref, sc_ref, o_ref, m1_s, *, t):
    group = pl.program_id(1)
    tile = 256
    rows_per_tile = tile // GRID_W
    n_rows = t // GRID_W
    halo = max(POOL_WINDOWS) // 2 * GRID_W
    ti = lax.broadcasted_iota(I32, (tile, tile), 0)
    si = lax.broadcasted_iota(I32, (tile, tile), 1)
    tcol = lax.broadcasted_iota(I32, (tile, 1), 0)

    def run(win):
        lo = win // 2
        hi = win - lo
        off = si - ti
        band = jnp.where((ti // GRID_W == si // GRID_W) & (off >= -lo) & (off < hi), 1.0, 0.0).astype(BF16)
        cpos = tcol % GRID_W
        inv_c = 1.0 / (jnp.minimum(cpos + hi, GRID_W) - jnp.maximum(cpos - lo, 0)).astype(F32)
        m1_s[0:halo, :] = jnp.zeros((halo, m1_s.shape[1]), F32)
        m1_s[halo + t:2 * halo + t, :] = jnp.zeros((halo, m1_s.shape[1]), F32)

        def along_w(r, carry):
            rows = pl.multiple_of(r * tile, tile)
            x = u_ref[0, pl.ds(rows, tile), :]
            tot = sum(_dg(band, p, _NN) for p in _pieces(x, 3))
            m1_s[pl.ds(rows + halo, tile), :] = tot * inv_c
            return carry

        lax.fori_loop(0, t // tile, along_w, 0, unroll=4)
        w = w_ref[0]
        scale = sc_ref[...]
        gate = gt_ref[0]

        def along_h(r, carry):
            rows = pl.multiple_of(r * tile, tile)
            acc = None
            for o in range(-lo, hi):
                term = m1_s[pl.ds(rows + halo + o * GRID_W, tile), :]
                acc = term if acc is None else acc + term
            rpos = r * rows_per_tile + tcol // GRID_W
            inv_r = 1.0 / (jnp.minimum(rpos + hi, n_rows) - jnp.maximum(rpos - lo, 0)).astype(F32)
            x = u_ref[0, pl.ds(rows, tile), :]
            y = _mmp(acc * inv_r - x, w, 2, 2) * scale
            o_ref[0, pl.ds(rows, tile), :] = h_ref[0, pl.ds(rows, tile), :] + gate * y
            return carry

        lax.fori_loop(0, t // tile, along_h, 0, unroll=4)

    for gi, win in enumerate(POOL_WINDOWS):
        pl.when(group == gi)(functools.partial(run, win))


def _pool_mixer(u, h, gate, w_grp, scale):
    b, t, d = u.shape
    groups = len(POOL_WINDOWS)
    gd = d // groups
    halo = max(POOL_WINDOWS) // 2 * GRID_W
    blk = pl.BlockSpec((1, t, gd), lambda i, g: (i, 0, g))
    return pl.pallas_call(
        functools.partial(_pool_kernel, t=t),
        out_shape=jax.ShapeDtypeStruct((b, t, d), F32),
        grid=(b, groups),
        in_specs=[blk, blk, pl.BlockSpec((1, 1, gd), lambda i, g: (i, 0, g)),
                  pl.BlockSpec((1, gd, gd), lambda i, g: (g, 0, 0)),
                  pl.BlockSpec((1, gd), lambda i, g: (0, g))],
        out_specs=blk,
        scratch_shapes=[pltpu.VMEM((t + 2 * halo, gd), F32)],
        compiler_params=_cparams(("parallel", "parallel")),
        name="pool_mixer",
    )(u, h, gate, w_grp, scale.reshape(1, d))


def _exclusive_count(mask, tri, ones):
    n = mask.shape[1] // LANES
    run = jnp.zeros((mask.shape[0], LANES), F32)
    maskf = jnp.where(mask, 1.0, 0.0)
    tiles = []
    for j in range(n):
        m = maskf[:, j * LANES:(j + 1) * LANES].astype(BF16)
        tiles.append(_dg(m, tri, _NN) + run)
        run = run + _dg(m, ones, _NN)
    return jnp.concatenate(tiles, axis=1), run


def _route_kernel(lgt_ref, o_ref, slot_s, tab_s, *, t, cap):
    e = lgt_ref.shape[1]
    lg = lgt_ref[0]
    ex = jnp.exp(lg - jnp.max(lg, axis=0, keepdims=True))
    aff = ex / jnp.sum(ex, axis=0, keepdims=True)

    def count_ge(thr):
        return jnp.sum(jnp.where(aff >= thr, 1.0, 0.0), axis=1, keepdims=True)

    n_exp = 126.0

    def pow2(k):
        return jnp.where(k >= n_exp, 0.0, jnp.exp2(-k))

    def search_exponent(i, carry):
        k_hi, k_lo = carry
        k_mid = jnp.floor(0.5 * (k_hi + k_lo))
        ok = count_ge(pow2(k_mid)) >= cap
        return jnp.where(ok, k_hi, k_mid), jnp.where(ok, k_mid, k_lo)

    k_hi, k_lo = lax.fori_loop(0, 8, search_exponent,
                               (jnp.full((e, 1), -1.0, F32), jnp.full((e, 1), n_exp, F32)))

    def search_value(i, carry):
        lo, hi = carry
        mid = 0.5 * (lo + hi)
        ok = count_ge(mid) >= cap
        return jnp.where(ok, mid, lo), jnp.where(ok, hi, mid)

    lo, hi = lax.fori_loop(0, 26, search_value, (pow2(k_lo), pow2(k_hi)))
    above = aff >= hi
    equal = (aff >= lo) & (aff < hi)
    ri = lax.broadcasted_iota(I32, (LANES, LANES), 0)
    ci = lax.broadcasted_iota(I32, (LANES, LANES), 1)
    tri = jnp.where(ri < ci, 1.0, 0.0).astype(BF16)
    ones = jnp.ones((LANES, LANES), BF16)
    n_above = jnp.sum(jnp.where(above, 1.0, 0.0), axis=1, keepdims=True)
    eq_rank, _ = _exclusive_count(equal, tri, ones)
    sel = above | (equal & (eq_rank < cap - n_above))
    slot, _ = _exclusive_count(sel, tri, ones)
    slot = jnp.where(sel, slot, -1.0)
    for x in range(e):
        slot_s[x] = slot[x:x + 1, :]

    tok = lax.broadcasted_iota(I32, (SUBLANES, t), 1)
    sub = lax.broadcasted_iota(I32, (SUBLANES, t), 0)
    head_rows = jnp.where(sub == 0, (tok // INDEX_BASE).astype(F32),
                          jnp.where(sub == 1, (tok % INDEX_BASE).astype(F32), 0.0))
    pad_rows = jnp.zeros((ROUTE_ROWS - SUBLANES - 3 * e, t), F32)
    tab_s[...] = jnp.concatenate([head_rows] + [p.astype(F32) for p in _pieces(aff, 3)] + [pad_rows],
                                 axis=0).astype(BF16)

    s_iota = lax.broadcasted_iota(I32, (cap, t), 0).astype(F32)

    def compact(x, carry):
        onehot = jnp.where(s_iota == slot_s[x], 1.0, 0.0).astype(BF16)
        o_ref[0, x] = _dg(tab_s[...], onehot, _NT)
        return carry

    lax.fori_loop(0, e, compact, 0)


def _route(lgt, cap):
    b, e, t = lgt.shape
    assert SUBLANES + 3 * e <= ROUTE_ROWS
    packed = pl.pallas_call(
        functools.partial(_route_kernel, t=t, cap=cap),
        out_shape=jax.ShapeDtypeStruct((b, e, ROUTE_ROWS, cap), F32),
        grid=(b,),
        in_specs=[pl.BlockSpec((1, e, t), lambda i: (i, 0, 0))],
        out_specs=pl.BlockSpec((1, e, ROUTE_ROWS, cap), lambda i: (i, 0, 0, 0)),
        scratch_shapes=[pltpu.VMEM((e, 1, t), F32), pltpu.VMEM((ROUTE_ROWS, t), BF16)],
        compiler_params=_cparams(("parallel",)),
        name="route_topk",
    )(lgt)
    idx = (packed[:, :, 0, :] * INDEX_BASE + packed[:, :, 1, :]).astype(I32)
    pieces = packed[:, :, SUBLANES:SUBLANES + 3 * e, :].reshape(b, e, 3, e, cap).sum(axis=2)
    gates = jnp.take_along_axis(pieces, jnp.arange(e).reshape(1, e, 1, 1), axis=2)[:, :, 0, :]
    return idx, gates


def _ffn_kernel(idx_ref, idx_next_ref, uf_hbm, wg_ref, wu_ref, wd_ref, ys_ref,
                x_s, xb_s, wg_s, wu_s, wd_s, sem, *, rows, tm):
    x, f = pl.program_id(0), pl.program_id(1)
    n_x, n_f = pl.num_programs(0), pl.num_programs(1)
    slot = x % 2
    n_tiles = rows // tm
    per_tile = rows // (n_f * n_tiles)

    def row_copy(token, r, s):
        return pltpu.make_async_copy(uf_hbm.at[pl.ds(token, 1), :], x_s.at[s, pl.ds(r, 1), :], sem.at[s])

    def all_rows(s):
        return pltpu.make_async_copy(uf_hbm.at[pl.ds(0, rows), :], x_s.at[s], sem.at[s])

    @pl.when((x == 0) & (f == 0))
    def _():
        def issue(r, carry):
            row_copy(idx_ref[0, 0, r], r, 0).start()
            return carry

        lax.fori_loop(0, rows, issue, 0, unroll=8)

    @pl.when(f == 0)
    def _():
        all_rows(slot).wait()

        def cast(r, carry):
            rr = pl.ds(pl.multiple_of(r * tm, tm), tm)
            xb_s[rr, :] = x_s[slot, rr, :].astype(BF16)
            return carry

        lax.fori_loop(0, n_tiles, cast, 0)

    wg_s[...] = wg_ref[0, 0].astype(BF16)
    wu_s[...] = wu_ref[0, 0].astype(BF16)
    wd_s[...] = wd_ref[0, 0].astype(BF16)

    def body(m, carry):
        base = (f * n_tiles + m) * per_tile
        for j in range(per_tile):
            row_copy(idx_next_ref[0, 0, base + j], base + j, 1 - slot).start()
        rr = pl.ds(pl.multiple_of(m * tm, tm), tm)
        xt = xb_s[rr, :]
        g = jnp.dot(xt, wg_s[...], preferred_element_type=F32)
        u = jnp.dot(xt, wu_s[...], preferred_element_type=F32)
        hid = (_silu(g) * u).astype(BF16)
        y = jnp.dot(hid, wd_s[...], preferred_element_type=F32)

        @pl.when(f == 0)
        def _():
            ys_ref[0, rr, :] = y

        @pl.when(f != 0)
        def _():
            ys_ref[0, rr, :] = ys_ref[0, rr, :] + y

        return carry

    lax.fori_loop(0, n_tiles, body, 0)

    @pl.when((x == n_x - 1) & (f == n_f - 1))
    def _():
        all_rows(1 - slot).wait()


def _expert_ffn(uf_flat, gidx, w_gate, w_up, w_down, layer):
    _, e, d, de = w_gate.shape
    rows = gidx.shape[-1]
    tf = 512
    tm = 512
    assert rows % tm == 0 and rows % ((rows // tm) * (de // tf)) == 0
    return pl.pallas_call(
        functools.partial(_ffn_kernel, rows=rows, tm=tm),
        out_shape=jax.ShapeDtypeStruct((e, rows, d), F32),
        grid=(e, de // tf),
        in_specs=[
            pl.BlockSpec((1, 1, rows), lambda x, f: (x, 0, 0), memory_space=pltpu.SMEM),
            pl.BlockSpec((1, 1, rows), lambda x, f: (jnp.minimum(x + 1, e - 1), 0, 0), memory_space=pltpu.SMEM),
            pl.BlockSpec(memory_space=pl.ANY),
            pl.BlockSpec((1, 1, d, tf), lambda x, f: (layer, x, 0, f)),
            pl.BlockSpec((1, 1, d, tf), lambda x, f: (layer, x, 0, f)),
            pl.BlockSpec((1, 1, tf, d), lambda x, f: (layer, x, f, 0)),
        ],
        out_specs=pl.BlockSpec((1, rows, d), lambda x, f: (x, 0, 0)),
        scratch_shapes=[pltpu.VMEM((2, rows, d), F32), pltpu.VMEM((rows, d), BF16),
                        pltpu.VMEM((d, tf), BF16), pltpu.VMEM((d, tf), BF16), pltpu.VMEM((tf, d), BF16),
                        pltpu.SemaphoreType.DMA((2,))],
        compiler_params=_cparams(("arbitrary", "arbitrary")),
        name="expert_ffn",
    )(gidx, gidx, uf_flat, w_gate, w_up, w_down)


def _combine_kernel(idx_ref, gate_ref, ys_ref, o_ref, *, cap, t):
    @pl.when(pl.program_id(1) == 0)
    def _():
        tile = 256

        def clear(r, carry):
            o_ref[0, pl.ds(pl.multiple_of(r * tile, tile), tile), :] = jnp.zeros((tile, o_ref.shape[2]), F32)
            return carry

        lax.fori_loop(0, t // tile, clear, 0)

    rows_per_group = SUBLANES

    def body(g, carry):
        base = pl.multiple_of(g * rows_per_group, rows_per_group)
        ys = ys_ref[0, pl.ds(base, rows_per_group), :]
        toks = [idx_ref[0, 0, base + k] for k in range(rows_per_group)]
        new = [o_ref[0, pl.ds(toks[k], 1), :] + gate_ref[0, 0, base + k] * ys[k:k + 1, :]
               for k in range(rows_per_group)]
        for k in range(rows_per_group):
            o_ref[0, pl.ds(toks[k], 1), :] = new[k]
        return carry

    lax.fori_loop(0, cap // rows_per_group, body, 0)


def _combine(ys, idx, gates, b, t):
    e, _, d = ys.shape
    cap = idx.shape[-1]
    smem = lambda: pl.BlockSpec((1, 1, cap), lambda i, x: (i * e + x, 0, 0), memory_space=pltpu.SMEM)
    return pl.pallas_call(
        functools.partial(_combine_kernel, cap=cap, t=t),
        out_shape=jax.ShapeDtypeStruct((b, t, d), F32),
        grid=(b, e),
        in_specs=[smem(), smem(), pl.BlockSpec((1, cap, d), lambda i, x: (x, i, 0))],
        out_specs=pl.BlockSpec((1, t, d), lambda i, x: (i, 0, 0)),
        compiler_params=_cparams(("parallel", "arbitrary")),
        name="moe_combine",
    )(idx.reshape(b * e, 1, cap), gates.reshape(b * e, 1, cap), ys)


def _ec_moe(uf, lgt, w_gate, w_up, w_down, layer):
    b, t, d = uf.shape
    e = w_gate.shape[1]
    cap = (EC_CAPACITY_FACTOR * t) // e
    idx, gates = _route(lgt, cap)
    gidx = (idx + (jnp.arange(b, dtype=I32) * t)[:, None, None]).transpose(1, 0, 2).reshape(e, 1, b * cap)
    ys = _expert_ffn(uf.reshape(b * t, d), gidx, w_gate, w_up, w_down, layer)
    return _combine(ys, idx, gates, b, t)


def kernel(x, c, ctx, c_ctx, w_mod, b_mod, norm_mix_g, norm_ffn_g, gdn_w_in, gdn_conv_w, gdn_a_log, gdn_dt_bias, gdn_o_gain, gdn_w_out, pool_w, pool_scale, moe_w_router, moe_w_gate, moe_w_up, moe_w_down, final_g):
    b, t, d = x.shape
    assert b < MOD_ROWS and t % (2 * CHUNK) == 0 and ctx.shape[1] % (2 * CHUNK) == 0
    assert d == HEADS * HEAD_DIM and w_mod.shape[0] == 2

    cc = jnp.zeros((MOD_ROWS, d), F32).at[:b].set(c).at[b].set(c_ctx)
    mod = _modulation(cc, w_mod, b_mod)

    def mod_rows(layer, row_slice, bcast):
        parts = []
        for i in range(N_MOD):
            m = mod[layer, row_slice, i * d:(i + 1) * d]
            parts.append(jnp.broadcast_to(m, (b, d)).reshape(b, 1, d) if bcast else m.reshape(b, 1, d))
        return parts

    sh_m, sc_m, gt_m, sh_f, sc_f, gt_f = mod_rows(0, slice(0, b), False)
    csh_m, csc_m = mod_rows(0, slice(b, b + 1), True)[:2]

    w_in = gdn_w_in[0]
    n_main = 4 * HEADS * HEAD_DIM
    w_main = w_in[:, :n_main].astype(BF16)
    w_ab = jnp.pad(w_in[:, n_main:], ((0, 0), (0, LANES - (w_in.shape[1] - n_main))))

    def gate_lanes(v):
        v = jnp.pad(v, ((0, 0), (0, GATE_LANES - HEADS))).reshape(1, N_DIRS * GATE_LANES)
        return jnp.pad(v, ((0, 0), (0, LANES - N_DIRS * GATE_LANES)))

    alog_l = gate_lanes(gdn_a_log[0])
    dtb_l = gate_lanes(gdn_dt_bias[0])
    p_lat, cp_lat = _gdn_project(x, norm_mix_g[0], sh_m, sc_m, w_main, w_ab, alog_l, dtb_l, tm=1024)
    p_ctx, cp_ctx = _gdn_project(ctx, norm_mix_g[0], csh_m, csc_m, w_main, w_ab, alog_l, dtb_l, tm=ctx.shape[1])
    s0 = jnp.zeros((b, HEADS, N_DIRS, HEAD_DIM, HEAD_DIM), F32)
    (s_ctx,) = _gdn_scan(p_ctx, cp_ctx, _row_gates(cp_ctx), gdn_conv_w[0], gdn_o_gain[0], s0, False)
    og, _ = _gdn_scan(p_lat, cp_lat, _row_gates(cp_lat), gdn_conv_w[0], gdn_o_gain[0], s_ctx, True)
    h, uf, lgt = _ffn_prep(x, norm_ffn_g[0], sh_f, sc_f, moe_w_router[0],
                                proj=(og, gdn_w_out[0].astype(BF16), gt_m))
    moe = _ec_moe(uf, lgt, moe_w_gate, moe_w_up, moe_w_down, 0)

    sh_m, sc_m, gt_m, sh_f1, sc_f1, gt_f1 = mod_rows(1, slice(0, b), False)
    h, u = _residual(h, moe, gt_f, norm_mix_g[1], sh_m, sc_m)
    h = _pool_mixer(u, h, gt_m, pool_w[0], pool_scale[0])
    uf, lgt = _ffn_prep(h, norm_ffn_g[1], sh_f1, sc_f1, moe_w_router[1])
    moe = _ec_moe(uf, lgt, moe_w_gate, moe_w_up, moe_w_down, 1)
    return _residual(h, moe, gt_f1, final_g)
```

```python
import functools

import jax
import jax.numpy as jnp
from jax import lax
from jax.experimental import pallas as pl
from jax.experimental.pallas import tpu as pltpu

F32, BF16, I32 = jnp.float32, jnp.bfloat16, jnp.int32

GRID_W = 64
N_DIRS = 2
HEADS = 8
HEAD_DIM = 128
CONV_W = 5
POOL_WINDOWS = (2, 4, 8, 16)
N_EXPERTS = 16
EC_CAPACITY_FACTOR = 2
N_MOD = 6
RMS_EPS = 1e-6

LANES = 128
SUBLANES = 8
VMEM_LIMIT_BYTES = 58 * 1024 * 1024

CHUNK = 128
MOD_ROWS = 8
GATE_LANES = 16
ROUTE_ROWS = 64
INDEX_BASE = 64


def _sigmoid(x):
    return 1.0 / (1.0 + jnp.exp(-x))


def _silu(x):
    return x * _sigmoid(x)


def _softplus(x):
    return jnp.maximum(x, 0.0) + jnp.log(1.0 + jnp.exp(-jnp.abs(x)))


def _pieces(x, n):
    out = []
    r = x
    for i in range(n):
        p = r.astype(BF16)
        out.append(p)
        if i + 1 < n:
            r = r - p.astype(F32)
    return out


_NN = (((1,), (0,)), ((), ()))
_NT = (((1,), (1,)), ((), ()))
_TN = (((0,), (0,)), ((), ()))


def _dg(a, b, dims):
    return lax.dot_general(a, b, dims, preferred_element_type=F32)


def _mm(a, b, dims=_NN):
    return _dg(a.astype(BF16), b.astype(BF16), dims)


def _mmp(a, b, na, nb, dims=_NN):
    pa = _pieces(a, na) if na > 1 else [a.astype(BF16)]
    pb = _pieces(b, nb) if nb > 1 else [b.astype(BF16)]
    order = max(na, nb)
    acc = None
    for i, x in enumerate(pa):
        for j, y in enumerate(pb):
            if i + j < order:
                t = _dg(x, y, dims)
                acc = t if acc is None else acc + t
    return acc


def _norm_mod(x, gain, shift, scale):
    ms = jnp.mean(x * x, axis=-1, keepdims=True)
    y = x * lax.rsqrt(ms + RMS_EPS) * gain
    return y * (1.0 + scale) + shift


def _cparams(sem):
    return pltpu.CompilerParams(dimension_semantics=sem, vmem_limit_bytes=VMEM_LIMIT_BYTES)


def _mod_kernel(cc_ref, w_ref, b_ref, o_ref):
    act = _silu(cc_ref[...])
    o_ref[0] = _mmp(act, w_ref[0], 2, 2) + b_ref[0]


def _modulation(cc, w_mod, b_mod):
    depth, d, nd = w_mod.shape
    return pl.pallas_call(
        _mod_kernel,
        out_shape=jax.ShapeDtypeStruct((depth, MOD_ROWS, nd), F32),
        grid=(depth, nd // d),
        in_specs=[
            pl.BlockSpec((MOD_ROWS, d), lambda l, j: (0, 0)),
            pl.BlockSpec((1, d, d), lambda l, j: (l, 0, j)),
            pl.BlockSpec((1, 1, d), lambda l, j: (l, 0, j)),
        ],
        out_specs=pl.BlockSpec((1, MOD_ROWS, d), lambda l, j: (l, 0, j)),
        compiler_params=_cparams(("parallel", "parallel")),
        name="modulation",
    )(cc, w_mod, b_mod.reshape(depth, 1, nd))


def _proj_kernel(x_ref, g_ref, sh_ref, sc_ref, w_ref, wab_ref, alog_ref, dtb_ref,
                 p_ref, cp_ref, u_s, *, tm):
    @pl.when(pl.program_id(2) == 0)
    def _():
        gain = g_ref[...]
        shift = sh_ref[0]
        scale = sc_ref[0]
        lane = lax.broadcasted_iota(I32, (CHUNK, LANES), 1)
        row = lax.broadcasted_iota(I32, (CHUNK, CHUNK), 0)
        col = lax.broadcasted_iota(I32, (CHUNK, CHUNK), 1)
        summing = jnp.concatenate([jnp.where(col <= row, 1.0, 0.0), jnp.where(col >= row, 1.0, 0.0),
                                   jnp.ones((CHUNK, CHUNK), F32)], axis=0).astype(BF16)
        is_decay = (lane % GATE_LANES < HEADS) & (lane < N_DIRS * GATE_LANES)
        n_chunks = tm // CHUNK
        group = 4 if n_chunks % 4 == 0 else 2
        wab = wab_ref[...]

        def body(r, carry):
            rows = [pl.ds(pl.multiple_of((r * group + k) * CHUNK, CHUNK), CHUNK) for k in range(group)]
            us = [_norm_mod(x_ref[0, rr, :], gain, shift, scale) for rr in rows]
            for rr, u in zip(rows, us):
                u_s[rr, :] = u.astype(BF16)
            abs_ = [_mmp(u, wab, 2, 2) for u in us]
            gs = [jnp.where(is_decay, -jnp.exp(alog_ref[...]) * _softplus(ab + dtb_ref[...]), 0.0) for ab in abs_]
            sums = [sum(_dg(summing, x, _NN) for x in _pieces(g, 3)) for g in gs]
            for rr, ab, s in zip(rows, abs_, sums):
                gc = jnp.where(lane < GATE_LANES, s[:CHUNK], s[CHUNK:2 * CHUNK])
                pk = jnp.where(is_decay, gc, _sigmoid(ab))
                pk = jnp.where(lane < N_DIRS * GATE_LANES, pk,
                               pltpu.roll(s[2 * CHUNK:], N_DIRS * GATE_LANES, axis=1))
                cp_ref[0, rr, :] = jnp.where(lane < 2 * N_DIRS * GATE_LANES, pk, 0.0)
            return carry

        lax.fori_loop(0, n_chunks // group, body, 0)

    p_ref[0] = jnp.dot(u_s[...], w_ref[...], preferred_element_type=F32)


def _gdn_project(x, gain, shift, scale, w_main, w_ab, alog_l, dtb_l, tm):
    b, t, d = x.shape
    n_main = w_main.shape[1]
    tn = 1024
    return pl.pallas_call(
        functools.partial(_proj_kernel, tm=tm),
        out_shape=(jax.ShapeDtypeStruct((b, t, n_main), F32),
                   jax.ShapeDtypeStruct((b, t, LANES), F32)),
        grid=(b, t // tm, n_main // tn),
        in_specs=[
            pl.BlockSpec((1, tm, d), lambda i, m, j: (i, m, 0)),
            pl.BlockSpec((1, d), lambda i, m, j: (0, 0)),
            pl.BlockSpec((1, 1, d), lambda i, m, j: (i, 0, 0)),
            pl.BlockSpec((1, 1, d), lambda i, m, j: (i, 0, 0)),
            pl.BlockSpec((d, tn), lambda i, m, j: (0, j)),
            pl.BlockSpec((d, LANES), lambda i, m, j: (0, 0)),
            pl.BlockSpec((1, LANES), lambda i, m, j: (0, 0)),
            pl.BlockSpec((1, LANES), lambda i, m, j: (0, 0)),
        ],
        out_specs=(pl.BlockSpec((1, tm, tn), lambda i, m, j: (i, m, j)),
                   pl.BlockSpec((1, tm, LANES), lambda i, m, j: (i, m, 0))),
        scratch_shapes=[pltpu.VMEM((tm, d), BF16)],
        compiler_params=_cparams(("parallel", "parallel", "arbitrary")),
        name="gdn_project",
    )(x, gain.reshape(1, d), shift, scale, w_main, w_ab, alog_l, dtb_l)


def _neumann_inverses(mats, eye):
    def tile(hi, lo):
        return jnp.concatenate([jnp.concatenate([hi, lo], axis=1),
                                jnp.concatenate([hi, jnp.zeros_like(hi)], axis=1)], axis=0)

    def fold(r):
        return r[:, :CHUNK] + r[:, CHUNK:]

    ps = [eye - a for a in mats]
    halves = [_pieces(a, 2) for a in mats]
    bs = [fold(_dg(jnp.concatenate(h, axis=1), tile(*h), _NN)) for h in halves]
    levels = CHUNK.bit_length() - 1
    for lvl in range(1, levels):
        b_halves = [_pieces(b, 2) for b in bs]
        p_halves = [_pieces(p, 2) for p in ps]
        wts = [tile(*h) for h in b_halves]
        if lvl + 1 < levels:
            lhs = [jnp.concatenate([jnp.concatenate(ph, axis=1), jnp.concatenate(bh, axis=1)], axis=0)
                   for ph, bh in zip(p_halves, b_halves)]
            pbs = [fold(_dg(x, wt, _NN)) for x, wt in zip(lhs, wts)]
            ps = [p + pb[:CHUNK] for p, pb in zip(ps, pbs)]
            bs = [pb[CHUNK:] for pb in pbs]
        else:
            ps = [p + fold(_dg(jnp.concatenate(ph, axis=1), wt, _NN)) for p, ph, wt in zip(ps, p_halves, wts)]
    return ps


def _gdn_kernel(qp_ref, kp_ref, vp_ref, zp_ref, cp_ref, rp_ref, cwq_ref, cwk_ref, cwv_ref,
                gain_ref, s0_ref, *rest, t, with_out):
    if with_out:
        og_ref, sfin_ref = rest[:2]
        pad_s, q_s, k_s, v_s, of_s, ob_s, st_s, mq_s, n_s, gt_s = rest[2:]
    else:
        (sfin_ref,) = rest[:1]
        pad_s, q_s, k_s, v_s, of_s, ob_s, st_s, mq_s, n_s, gt_s = rest[1:]
    head = pl.program_id(1)
    n_chunks = t // CHUNK
    halo = SUBLANES

    row = lax.broadcasted_iota(I32, (CHUNK, CHUNK), 0)
    col = lax.broadcasted_iota(I32, (CHUNK, CHUNK), 1)

    tile = 256
    zeros_halo = jnp.zeros((halo, HEAD_DIM), F32)
    streams = ((qp_ref, cwq_ref, q_s, "q"), (kp_ref, cwk_ref, k_s, "k"), (vp_ref, cwv_ref, v_s, "v"))
    for n, (src_ref, _, _, _) in enumerate(streams):
        pad_s[n, 0:halo, :] = zeros_halo
        pad_s[n, t + halo:t + 2 * halo, :] = zeros_halo

        def fill(r, carry, src_ref=src_ref, n=n):
            rows = pl.multiple_of(r * tile, tile)
            pad_s[n, pl.ds(rows + halo, tile), :] = src_ref[0, pl.ds(rows, tile), :]
            return carry

        lax.fori_loop(0, t // tile, fill, 0)
    conv_w = [cw_ref[...] for _, cw_ref, _, _ in streams]

    def conv_tiles(first, count):
        for tt in range(count):
            rows = pl.multiple_of((first + tt) * tile, tile)
            for n, (_, _, dst, mode) in enumerate(streams):
                win = pad_s[n, pl.ds(rows, tile + 2 * halo), :]
                acc = None
                for j in range(CONV_W):
                    off = halo + j - CONV_W // 2
                    term = conv_w[n][j:j + 1, :] * win[off:off + tile, :]
                    acc = term if acc is None else acc + term
                y = _silu(acc)
                if mode != "v":
                    y = y * lax.rsqrt(jnp.sum(y * y, axis=-1, keepdims=True) + RMS_EPS)
                if mode == "q":
                    y = y * (HEAD_DIM ** -0.5)
                dst[pl.ds(rows, tile), :] = y

    st_s[...] = s0_ref[0, 0]
    eye = jnp.where(row == col, 1.0, 0.0)
    masks = ((col <= row, col < row), (col >= row, col > row))
    o_s = (of_s, ob_s)

    group = 4 if n_chunks % 4 == 0 else 2

    def prepare(i):
        chains = []
        for g in range(group):
            c = i * group + g
            rows = pl.ds(pl.multiple_of(c * CHUNK, CHUNK), CHUNK)
            q = q_s[rows, :]
            k = k_s[rows, :]
            v = v_s[rows, :]
            cp = pltpu.roll(cp_ref[0, rows, :], LANES - head, axis=1)
            rp = rp_ref[0, 0, c]
            kq = _mm(jnp.concatenate([k, q], axis=0), k, _NT)
            for d in range(N_DIRS):
                lane = d * GATE_LANES
                bcast = lambda j: jnp.broadcast_to(cp[:, j:j + 1], (CHUNK, HEAD_DIM))
                gc, beta, gtot = bcast(lane), bcast(lane + HEADS), bcast(lane + N_DIRS * GATE_LANES)
                grow = jnp.broadcast_to(rp[d:d + 1, :], (CHUNK, CHUNK))
                incl, strict = masks[d]
                decay = jnp.where(incl, jnp.exp(jnp.where(incl, gc - grow, 0.0)), 0.0)
                e_in = jnp.exp(gc)
                chains.append(dict(
                    d=d, c=c, rows=rows, qd=q * e_in, gtot=gtot,
                    a=jnp.where(strict, beta * kq[:CHUNK] * decay, 0.0),
                    att=kq[CHUNK:] * decay,
                    rhs=jnp.concatenate([v * beta, k * (beta * e_in)], axis=1),
                    kd=k * jnp.exp(gtot - gc)))
        tinvs = _neumann_inverses([ch["a"] for ch in chains], eye)
        uws = [_mm(tinv, ch["rhs"]) for tinv, ch in zip(tinvs, chains)]
        rs = [_mm(jnp.concatenate([ch["att"], ch["kd"].T], axis=0), uw) for ch, uw in zip(chains, uws)]
        for ch, r in zip(chains, rs):
            d, c, rows = ch["d"], ch["c"], ch["rows"]
            mq_s[d, c] = jnp.concatenate([-r[CHUNK:, HEAD_DIM:], ch["qd"] - r[:CHUNK, HEAD_DIM:]],
                                         axis=0).astype(BF16)
            n_s[d, rows, :] = r[CHUNK:, :HEAD_DIM]
            o_s[d][rows, :] = r[:CHUNK, :HEAD_DIM]
            gt_s[d, c] = jnp.exp(ch["gtot"][:SUBLANES])

    n_groups = n_chunks // group
    tiles_per_group = group * CHUNK // tile
    conv_tiles(0, tiles_per_group)

    def prepare_and_conv(i, carry):
        prepare(i)
        conv_tiles((i + 1) * tiles_per_group, tiles_per_group)
        return carry

    lax.fori_loop(0, n_groups - 1, prepare_and_conv, 0)
    prepare(n_groups - 1)

    def scan(i, carry):
        for d in range(N_DIRS):
            c = i if d == 0 else n_chunks - 1 - i
            rows = pl.ds(pl.multiple_of(c * CHUNK, CHUNK), CHUNK)
            state = st_s[d]
            r = jnp.dot(mq_s[d, c], state.astype(BF16), preferred_element_type=F32)
            gt = jnp.broadcast_to(gt_s[d, c][0:1, :], (HEAD_DIM, HEAD_DIM))
            st_s[d] = state * gt + r[:CHUNK] + n_s[d, rows, :]
            o_s[d][rows, :] = o_s[d][rows, :] + r[CHUNK:]
        return carry

    lax.fori_loop(0, n_chunks, scan, 0, unroll=2)
    sfin_ref[0, 0] = st_s[...]

    if with_out:
        gain = gain_ref[...]

        def emit(r, carry):
            rows = pl.ds(pl.multiple_of(r * tile, tile), tile)
            o = of_s[rows, :] + ob_s[rows, :]
            o = o * lax.rsqrt(jnp.mean(o * o, axis=-1, keepdims=True) + RMS_EPS) * gain
            og_ref[0, rows, :] = (o * _silu(zp_ref[0, rows, :])).astype(BF16)
            return carry

        lax.fori_loop(0, t // tile, emit, 0)


def _gdn_scan(p, cp, rp, conv_w, o_gain, s0, with_out):
    b, t, _ = p.shape
    hd = HEAD_DIM
    col_spec = lambda off: pl.BlockSpec((1, t, hd), lambda i, h, off=off: (i, 0, off + h))
    cw_spec = lambda off: pl.BlockSpec((CONV_W, hd), lambda i, h, off=off: (0, off + h))
    state_spec = pl.BlockSpec((1, 1, N_DIRS, hd, hd), lambda i, h: (i, h, 0, 0, 0))
    out_shape = [jax.ShapeDtypeStruct((b, HEADS, N_DIRS, hd, hd), F32)]
    out_specs = [state_spec]
    if with_out:
        out_shape = [jax.ShapeDtypeStruct((b, t, HEADS * hd), BF16)] + out_shape
        out_specs = [pl.BlockSpec((1, t, hd), lambda i, h: (i, 0, h))] + out_specs
    halo = SUBLANES
    res = pl.pallas_call(
        functools.partial(_gdn_kernel, t=t, with_out=with_out),
        out_shape=tuple(out_shape),
        grid=(b, HEADS),
        in_specs=[
            col_spec(0), col_spec(HEADS), col_spec(2 * HEADS), col_spec(3 * HEADS),
            pl.BlockSpec((1, t, LANES), lambda i, h: (i, 0, 0)),
            pl.BlockSpec((1, 1, t // CHUNK, SUBLANES, CHUNK), lambda i, h: (i, h, 0, 0, 0)),
            cw_spec(0), cw_spec(HEADS), cw_spec(2 * HEADS),
            pl.BlockSpec((1, hd), lambda i, h: (0, 0)),
            state_spec,
        ],
        out_specs=tuple(out_specs),
        scratch_shapes=[
            pltpu.VMEM((3, t + 2 * halo, hd), F32),
            pltpu.VMEM((t, hd), F32), pltpu.VMEM((t, hd), F32), pltpu.VMEM((t, hd), F32),
            pltpu.VMEM((t, hd), F32), pltpu.VMEM((t, hd), F32),
            pltpu.VMEM((N_DIRS, hd, hd), F32),
            pltpu.VMEM((N_DIRS, t // CHUNK, 2 * CHUNK, hd), BF16),
            pltpu.VMEM((N_DIRS, t, hd), F32),
            pltpu.VMEM((N_DIRS, t // CHUNK, SUBLANES, hd), F32),
        ],
        compiler_params=_cparams(("parallel", "arbitrary")),
        name="gdn_scan_out" if with_out else "gdn_scan_ctx",
    )(p, p, p, p, cp, rp, conv_w, conv_w, conv_w, o_gain.reshape(1, hd), s0)
    return res


def _row_gates(cp):
    b, t, _ = cp.shape
    g = cp[:, :, :N_DIRS * GATE_LANES].reshape(b, t // CHUNK, CHUNK, N_DIRS, GATE_LANES)[..., :HEADS]
    g = g.transpose(0, 4, 1, 3, 2)
    return jnp.pad(g, ((0, 0), (0, 0), (0, 0), (0, SUBLANES - N_DIRS), (0, 0)))


def _ffn_prep_kernel(*refs, tm, with_proj):
    if with_proj:
        (og_ref, wo_ref, x_ref, gt_ref, g_ref, sh_ref, sc_ref, wrt_ref,
         h_ref, uf_ref, lgt_ref) = refs
    else:
        (x_ref, g_ref, sh_ref, sc_ref, wrt_ref, uf_ref, lgt_ref) = refs
    gain = g_ref[...]
    shift = sh_ref[0]
    scale = sc_ref[0]
    sub = 256

    for r in range(tm // sub):
        rows = slice(r * sub, (r + 1) * sub)
        h = x_ref[0, rows, :]
        if with_proj:
            h = h + gt_ref[0] * jnp.dot(og_ref[0, rows, :], wo_ref[...], preferred_element_type=F32)
            h_ref[0, rows, :] = h
        u = _norm_mod(h, gain, shift, scale)
        uf_ref[0, rows, :] = u
        lgt_ref[0, :, rows] = _mmp(wrt_ref[...], u, 2, 2, _NT)


def _ffn_prep(x, gain, shift, scale, w_router, proj=None):
    b, t, d = x.shape
    e = w_router.shape[1]
    tm = 512
    wrt = w_router.T
    row_spec = pl.BlockSpec((1, tm, d), lambda i, m: (i, m, 0))
    vec_spec = pl.BlockSpec((1, 1, d), lambda i, m: (i, 0, 0))
    common_in = [pl.BlockSpec((1, d), lambda i, m: (0, 0)), vec_spec, vec_spec,
                 pl.BlockSpec((e, d), lambda i, m: (0, 0))]
    common_args = [gain.reshape(1, d), shift, scale, wrt]
    outs = [jax.ShapeDtypeStruct((b, t, d), F32), jax.ShapeDtypeStruct((b, e, t), F32)]
    out_specs = [row_spec, pl.BlockSpec((1, e, tm), lambda i, m: (i, 0, m))]
    if proj is not None:
        og, w_out, gate = proj
        in_specs = [row_spec, pl.BlockSpec((d, d), lambda i, m: (0, 0)), row_spec, vec_spec] + common_in
        args = [og, w_out, x, gate] + common_args
        outs = [jax.ShapeDtypeStruct((b, t, d), F32)] + outs
        out_specs = [row_spec] + out_specs
    else:
        in_specs = [row_spec] + common_in
        args = [x] + common_args
    return pl.pallas_call(
        functools.partial(_ffn_prep_kernel, tm=tm, with_proj=proj is not None),
        out_shape=tuple(outs), grid=(b, t // tm), in_specs=in_specs, out_specs=tuple(out_specs),
        compiler_params=_cparams(("parallel", "parallel")),
        name="ffn_prep_proj" if proj is not None else "ffn_prep",
    )(*args)


def _residual_kernel(h_ref, m_ref, gt_ref, g_ref, *refs, tm, mode):
    sub = 256
    gain = g_ref[...]

    def body(r, carry):
        rows = pl.ds(pl.multiple_of(r * sub, sub), sub)
        h = h_ref[0, rows, :] + gt_ref[0] * m_ref[0, rows, :]
        if mode == "mix":
            sh_ref, sc_ref, ho_ref, u_ref = refs
            ho_ref[0, rows, :] = h
            u_ref[0, rows, :] = _norm_mod(h, gain, sh_ref[0], sc_ref[0])
        else:
            (o_ref,) = refs
            ms = jnp.mean(h * h, axis=-1, keepdims=True)
            o_ref[0, rows, :] = h * lax.rsqrt(ms + RMS_EPS) * gain
        return carry

    lax.fori_loop(0, tm // sub, body, 0)


def _residual(h, m, gate, gain, shift=None, scale=None):
    b, t, d = h.shape
    tm = 512
    row_spec = pl.BlockSpec((1, tm, d), lambda i, j: (i, j, 0))
    vec_spec = pl.BlockSpec((1, 1, d), lambda i, j: (i, 0, 0))
    in_specs = [row_spec, row_spec, vec_spec, pl.BlockSpec((1, d), lambda i, j: (0, 0))]
    args = [h, m, gate, gain.reshape(1, d)]
    if shift is not None:
        mode = "mix"
        in_specs += [vec_spec, vec_spec]
        args += [shift, scale]
        outs = (jax.ShapeDtypeStruct((b, t, d), F32), jax.ShapeDtypeStruct((b, t, d), F32))
        out_specs = (row_spec, row_spec)
    else:
        mode = "final"
        outs = jax.ShapeDtypeStruct((b, t, d), F32)
        out_specs = row_spec
    return pl.pallas_call(
        functools.partial(_residual_kernel, tm=tm, mode=mode),
        out_shape=outs, grid=(b, t // tm), in_specs=in_specs, out_specs=out_specs,
        compiler_params=_cparams(("parallel", "parallel")),
        name="residual_" + mode,
    )(*args)


def _pool_kernel(u_ref, h_ref, gt_ref, w_ref, sc_ref, o_ref, m1_s, *, t):
    group = pl.program_id(1)
    tile = 256
    rows_per_tile = tile // GRID_W
    n_rows = t // GRID_W
    halo = max(POOL_WINDOWS) // 2 * GRID_W
    ti = lax.broadcasted_iota(I32, (tile, tile), 0)
    si = lax.broadcasted_iota(I32, (tile, tile), 1)
    tcol = lax.broadcasted_iota(I32, (tile, 1), 0)

    def run(win):
        lo = win // 2
        hi = win - lo
        off = si - ti
        band = jnp.where((ti // GRID_W == si // GRID_W) & (off >= -lo) & (off < hi), 1.0, 0.0).astype(BF16)
        cpos = tcol % GRID_W
        inv_c = 1.0 / (jnp.minimum(cpos + hi, GRID_W) - jnp.maximum(cpos - lo, 0)).astype(F32)
        m1_s[0:halo, :] = jnp.zeros((halo, m1_s.shape[1]), F32)
        m1_s[halo + t:2 * halo + t, :] = jnp.zeros((halo, m1_s.shape[1]), F32)

        def along_w(r, carry):
            rows = pl.multiple_of(r * tile, tile)
            x = u_ref[0, pl.ds(rows, tile), :]
            tot = sum(_dg(band, p, _NN) for p in _pieces(x, 3))
            m1_s[pl.ds(rows + halo, tile), :] = tot * inv_c
            return carry

        lax.fori_loop(0, t // tile, along_w, 0, unroll=4)
        w = w_ref[0]
        scale = sc_ref[...]
        gate = gt_ref[0]

        def along_h(r, carry):
            rows = pl.multiple_of(r * tile, tile)
            acc = None
            for o in range(-lo, hi):
                term = m1_s[pl.ds(rows + halo + o * GRID_W, tile), :]
                acc = term if acc is None else acc + term
            rpos = r * rows_per_tile + tcol // GRID_W
            inv_r = 1.0 / (jnp.minimum(rpos + hi, n_rows) - jnp.maximum(rpos - lo, 0)).astype(F32)
            x = u_ref[0, pl.ds(rows, tile), :]
            y = _mmp(acc * inv_r - x, w, 2, 2) * scale
            o_ref[0, pl.ds(rows, tile), :] = h_ref[0, pl.ds(rows, tile), :] + gate * y
            return carry

        lax.fori_loop(0, t // tile, along_h, 0, unroll=4)

    for gi, win in enumerate(POOL_WINDOWS):
        pl.when(group == gi)(functools.partial(run, win))


def _pool_mixer(u, h, gate, w_grp, scale):
    b, t, d = u.shape
    groups = len(POOL_WINDOWS)
    gd = d // groups
    halo = max(POOL_WINDOWS) // 2 * GRID_W
    blk = pl.BlockSpec((1, t, gd), lambda i, g: (i, 0, g))
    return pl.pallas_call(
        functools.partial(_pool_kernel, t=t),
        out_shape=jax.ShapeDtypeStruct((b, t, d), F32),
        grid=(b, groups),
        in_specs=[blk, blk, pl.BlockSpec((1, 1, gd), lambda i, g: (i, 0, g)),
                  pl.BlockSpec((1, gd, gd), lambda i, g: (g, 0, 0)),
                  pl.BlockSpec((1, gd), lambda i, g: (0, g))],
        out_specs=blk,
        scratch_shapes=[pltpu.VMEM((t + 2 * halo, gd), F32)],
        compiler_params=_cparams(("parallel", "parallel")),
        name="pool_mixer",
    )(u, h, gate, w_grp, scale.reshape(1, d))


def _exclusive_count(mask, tri, ones):
    n = mask.shape[1] // LANES
    run = jnp.zeros((mask.shape[0], LANES), F32)
    maskf = jnp.where(mask, 1.0, 0.0)
    tiles = []
    for j in range(n):
        m = maskf[:, j * LANES:(j + 1) * LANES].astype(BF16)
        tiles.append(_dg(m, tri, _NN) + run)
        run = run + _dg(m, ones, _NN)
    return jnp.concatenate(tiles, axis=1), run


def _route_kernel(lgt_ref, o_ref, slot_s, tab_s, *, t, cap):
    e = lgt_ref.shape[1]
    lg = lgt_ref[0]
    ex = jnp.exp(lg - jnp.max(lg, axis=0, keepdims=True))
    aff = ex / jnp.sum(ex, axis=0, keepdims=True)

    def count_ge(thr):
        return jnp.sum(jnp.where(aff >= thr, 1.0, 0.0), axis=1, keepdims=True)

    n_exp = 126.0

    def pow2(k):
        return jnp.where(k >= n_exp, 0.0, jnp.exp2(-k))

    def search_exponent(i, carry):
        k_hi, k_lo = carry
        k_mid = jnp.floor(0.5 * (k_hi + k_lo))
        ok = count_ge(pow2(k_mid)) >= cap
        return jnp.where(ok, k_hi, k_mid), jnp.where(ok, k_mid, k_lo)

    k_hi, k_lo = lax.fori_loop(0, 8, search_exponent,
                               (jnp.full((e, 1), -1.0, F32), jnp.full((e, 1), n_exp, F32)))

    def search_value(i, carry):
        lo, hi = carry
        mid = 0.5 * (lo + hi)
        ok = count_ge(mid) >= cap
        return jnp.where(ok, mid, lo), jnp.where(ok, hi, mid)

    lo, hi = lax.fori_loop(0, 26, search_value, (pow2(k_lo), pow2(k_hi)))
    above = aff >= hi
    equal = (aff >= lo) & (aff < hi)
    ri = lax.broadcasted_iota(I32, (LANES, LANES), 0)
    ci = lax.broadcasted_iota(I32, (LANES, LANES), 1)
    tri = jnp.where(ri < ci, 1.0, 0.0).astype(BF16)
    ones = jnp.ones((LANES, LANES), BF16)
    n_above = jnp.sum(jnp.where(above, 1.0, 0.0), axis=1, keepdims=True)
    eq_rank, _ = _exclusive_count(equal, tri, ones)
    sel = above | (equal & (eq_rank < cap - n_above))
    slot, _ = _exclusive_count(sel, tri, ones)
    slot = jnp.where(sel, slot, -1.0)
    for x in range(e):
        slot_s[x] = slot[x:x + 1, :]

    tok = lax.broadcasted_iota(I32, (SUBLANES, t), 1)
    sub = lax.broadcasted_iota(I32, (SUBLANES, t), 0)
    head_rows = jnp.where(sub == 0, (tok // INDEX_BASE).astype(F32),
                          jnp.where(sub == 1, (tok % INDEX_BASE).astype(F32), 0.0))
    pad_rows = jnp.zeros((ROUTE_ROWS - SUBLANES - 3 * e, t), F32)
    tab_s[...] = jnp.concatenate([head_rows] + [p.astype(F32) for p in _pieces(aff, 3)] + [pad_rows],
                                 axis=0).astype(BF16)

    s_iota = lax.broadcasted_iota(I32, (cap, t), 0).astype(F32)

    def compact(x, carry):
        onehot = jnp.where(s_iota == slot_s[x], 1.0, 0.0).astype(BF16)
        o_ref[0, x] = _dg(tab_s[...], onehot, _NT)
        return carry

    lax.fori_loop(0, e, compact, 0)


def _route(lgt, cap):
    b, e, t = lgt.shape
    assert SUBLANES + 3 * e <= ROUTE_ROWS
    packed = pl.pallas_call(
        functools.partial(_route_kernel, t=t, cap=cap),
        out_shape=jax.ShapeDtypeStruct((b, e, ROUTE_ROWS, cap), F32),
        grid=(b,),
        in_specs=[pl.BlockSpec((1, e, t), lambda i: (i, 0, 0))],
        out_specs=pl.BlockSpec((1, e, ROUTE_ROWS, cap), lambda i: (i, 0, 0, 0)),
        scratch_shapes=[pltpu.VMEM((e, 1, t), F32), pltpu.VMEM((ROUTE_ROWS, t), BF16)],
        compiler_params=_cparams(("parallel",)),
        name="route_topk",
    )(lgt)
    idx = (packed[:, :, 0, :] * INDEX_BASE + packed[:, :, 1, :]).astype(I32)
    pieces = packed[:, :, SUBLANES:SUBLANES + 3 * e, :].reshape(b, e, 3, e, cap).sum(axis=2)
    gates = jnp.take_along_axis(pieces, jnp.arange(e).reshape(1, e, 1, 1), axis=2)[:, :, 0, :]
    return idx, gates


def _ffn_kernel(idx_ref, idx_next_ref, uf_hbm, wg_ref, wu_ref, wd_ref, ys_ref,
                x_s, xb_s, wg_s, wu_s, wd_s, sem, *, rows, tm):
    x, f = pl.program_id(0), pl.program_id(1)
    n_x, n_f = pl.num_programs(0), pl.num_programs(1)
    slot = x % 2
    n_tiles = rows // tm
    per_tile = rows // (n_f * n_tiles)

    def row_copy(token, r, s):
        return pltpu.make_async_copy(uf_hbm.at[pl.ds(token, 1), :], x_s.at[s, pl.ds(r, 1), :], sem.at[s])

    def all_rows(s):
        return pltpu.make_async_copy(uf_hbm.at[pl.ds(0, rows), :], x_s.at[s], sem.at[s])

    @pl.when((x == 0) & (f == 0))
    def _():
        def issue(r, carry):
            row_copy(idx_ref[0, 0, r], r, 0).start()
            return carry

        lax.fori_loop(0, rows, issue, 0, unroll=8)

    @pl.when(f == 0)
    def _():
        all_rows(slot).wait()

        def cast(r, carry):
            rr = pl.ds(pl.multiple_of(r * tm, tm), tm)
            xb_s[rr, :] = x_s[slot, rr, :].astype(BF16)
            ys_ref[0, rr, :] = jnp.zeros((tm, ys_ref.shape[2]), F32)
            return carry

        lax.fori_loop(0, n_tiles, cast, 0)

    wg_s[...] = wg_ref[0, 0].astype(BF16)
    wu_s[...] = wu_ref[0, 0].astype(BF16)
    wd_s[...] = wd_ref[0, 0].astype(BF16)

    def body(m, carry):
        base = (f * n_tiles + m) * per_tile
        for j in range(per_tile):
            row_copy(idx_next_ref[0, 0, base + j], base + j, 1 - slot).start()
        rr = pl.ds(pl.multiple_of(m * tm, tm), tm)
        xt = xb_s[rr, :]
        g = jnp.dot(xt, wg_s[...], preferred_element_type=F32)
        u = jnp.dot(xt, wu_s[...], preferred_element_type=F32)
        hid = (_silu(g) * u).astype(BF16)
        ys_ref[0, rr, :] = ys_ref[0, rr, :] + jnp.dot(hid, wd_s[...], preferred_element_type=F32)
        return carry

    lax.fori_loop(0, n_tiles, body, 0, unroll=True)

    @pl.when((x == n_x - 1) & (f == n_f - 1))
    def _():
        all_rows(1 - slot).wait()


def _expert_ffn(uf_flat, gidx, w_gate, w_up, w_down, layer):
    _, e, d, de = w_gate.shape
    rows = gidx.shape[-1]
    tf = 512
    tm = 512
    assert rows % tm == 0 and rows % ((rows // tm) * (de // tf)) == 0
    return pl.pallas_call(
        functools.partial(_ffn_kernel, rows=rows, tm=tm),
        out_shape=jax.ShapeDtypeStruct((e, rows, d), F32),
        grid=(e, de // tf),
        in_specs=[
            pl.BlockSpec((1, 1, rows), lambda x, f: (x, 0, 0), memory_space=pltpu.SMEM),
            pl.BlockSpec((1, 1, rows), lambda x, f: (jnp.minimum(x + 1, e - 1), 0, 0), memory_space=pltpu.SMEM),
            pl.BlockSpec(memory_space=pl.ANY),
            pl.BlockSpec((1, 1, d, tf), lambda x, f: (layer, x, 0, f)),
            pl.BlockSpec((1, 1, d, tf), lambda x, f: (layer, x, 0, f)),
            pl.BlockSpec((1, 1, tf, d), lambda x, f: (layer, x, f, 0)),
        ],
        out_specs=pl.BlockSpec((1, rows, d), lambda x, f: (x, 0, 0)),
        scratch_shapes=[pltpu.VMEM((2, rows, d), F32), pltpu.VMEM((rows, d), BF16),
                        pltpu.VMEM((d, tf), BF16), pltpu.VMEM((d, tf), BF16), pltpu.VMEM((tf, d), BF16),
                        pltpu.SemaphoreType.DMA((2,))],
        compiler_params=_cparams(("arbitrary", "arbitrary")),
        name="expert_ffn",
    )(gidx, gidx, uf_flat, w_gate, w_up, w_down)


def _combine_kernel(idx_ref, gate_ref, ys_ref, o_ref, *, cap, t):
    @pl.when(pl.program_id(1) == 0)
    def _():
        tile = 256

        def clear(r, carry):
            o_ref[0, pl.ds(pl.multiple_of(r * tile, tile), tile), :] = jnp.zeros((tile, o_ref.shape[2]), F32)
            return carry

        lax.fori_loop(0, t // tile, clear, 0)

    rows_per_group = SUBLANES

    def body(g, carry):
        base = pl.multiple_of(g * rows_per_group, rows_per_group)
        ys = ys_ref[0, pl.ds(base, rows_per_group), :]
        toks = [idx_ref[0, 0, base + k] for k in range(rows_per_group)]
        new = [o_ref[0, pl.ds(toks[k], 1), :] + gate_ref[0, 0, base + k] * ys[k:k + 1, :]
               for k in range(rows_per_group)]
        for k in range(rows_per_group):
            o_ref[0, pl.ds(toks[k], 1), :] = new[k]
        return carry

    lax.fori_loop(0, cap // rows_per_group, body, 0)


def _combine(ys, idx, gates, b, t):
    e, _, d = ys.shape
    cap = idx.shape[-1]
    smem = lambda: pl.BlockSpec((1, 1, cap), lambda i, x: (i * e + x, 0, 0), memory_space=pltpu.SMEM)
    return pl.pallas_call(
        functools.partial(_combine_kernel, cap=cap, t=t),
        out_shape=jax.ShapeDtypeStruct((b, t, d), F32),
        grid=(b, e),
        in_specs=[smem(), smem(), pl.BlockSpec((1, cap, d), lambda i, x: (x, i, 0))],
        out_specs=pl.BlockSpec((1, t, d), lambda i, x: (i, 0, 0)),
        compiler_params=_cparams(("parallel", "arbitrary")),
        name="moe_combine",
    )(idx.reshape(b * e, 1, cap), gates.reshape(b * e, 1, cap), ys)


def _ec_moe(uf, lgt, w_gate, w_up, w_down, layer):
    b, t, d = uf.shape
    e = w_gate.shape[1]
    cap = (EC_CAPACITY_FACTOR * t) // e
    idx, gates = _route(lgt, cap)
    gidx = (idx + (jnp.arange(b, dtype=I32) * t)[:, None, None]).transpose(1, 0, 2).reshape(e, 1, b * cap)
    ys = _expert_ffn(uf.reshape(b * t, d), gidx, w_gate, w_up, w_down, layer)
    return _combine(ys, idx, gates, b, t)


def kernel(x, c, ctx, c_ctx, w_mod, b_mod, norm_mix_g, norm_ffn_g, gdn_w_in, gdn_conv_w, gdn_a_log, gdn_dt_bias, gdn_o_gain, gdn_w_out, pool_w, pool_scale, moe_w_router, moe_w_gate, moe_w_up, moe_w_down, final_g):
    b, t, d = x.shape
    assert b < MOD_ROWS and t % (2 * CHUNK) == 0 and ctx.shape[1] % (2 * CHUNK) == 0
    assert d == HEADS * HEAD_DIM and w_mod.shape[0] == 2

    cc = jnp.zeros((MOD_ROWS, d), F32).at[:b].set(c).at[b].set(c_ctx)
    mod = _modulation(cc, w_mod, b_mod)

    def mod_rows(layer, row_slice, bcast):
        parts = []
        for i in range(N_MOD):
            m = mod[layer, row_slice, i * d:(i + 1) * d]
            parts.append(jnp.broadcast_to(m, (b, d)).reshape(b, 1, d) if bcast else m.reshape(b, 1, d))
        return parts

    sh_m, sc_m, gt_m, sh_f, sc_f, gt_f = mod_rows(0, slice(0, b), False)
    csh_m, csc_m = mod_rows(0, slice(b, b + 1), True)[:2]

    w_in = gdn_w_in[0]
    n_main = 4 * HEADS * HEAD_DIM
    w_main = w_in[:, :n_main].astype(BF16)
    w_ab = jnp.pad(w_in[:, n_main:], ((0, 0), (0, LANES - (w_in.shape[1] - n_main))))

    def gate_lanes(v):
        v = jnp.pad(v, ((0, 0), (0, GATE_LANES - HEADS))).reshape(1, N_DIRS * GATE_LANES)
        return jnp.pad(v, ((0, 0), (0, LANES - N_DIRS * GATE_LANES)))

    alog_l = gate_lanes(gdn_a_log[0])
    dtb_l = gate_lanes(gdn_dt_bias[0])
    p_lat, cp_lat = _gdn_project(x, norm_mix_g[0], sh_m, sc_m, w_main, w_ab, alog_l, dtb_l, tm=1024)
    p_ctx, cp_ctx = _gdn_project(ctx, norm_mix_g[0], csh_m, csc_m, w_main, w_ab, alog_l, dtb_l, tm=ctx.shape[1])
    s0 = jnp.zeros((b, HEADS, N_DIRS, HEAD_DIM, HEAD_DIM), F32)
    (s_ctx,) = _gdn_scan(p_ctx, cp_ctx, _row_gates(cp_ctx), gdn_conv_w[0], gdn_o_gain[0], s0, False)
    og, _ = _gdn_scan(p_lat, cp_lat, _row_gates(cp_lat), gdn_conv_w[0], gdn_o_gain[0], s_ctx, True)
    h, uf, lgt = _ffn_prep(x, norm_ffn_g[0], sh_f, sc_f, moe_w_router[0],
                                proj=(og, gdn_w_out[0].astype(BF16), gt_m))
    moe = _ec_moe(uf, lgt, moe_w_gate, moe_w_up, moe_w_down, 0)

    sh_m, sc_m, gt_m, sh_f1, sc_f1, gt_f1 = mod_rows(1, slice(0, b), False)
    h, u = _residual(h, moe, gt_f, norm_mix_g[1], sh_m, sc_m)
    h = _pool_mixer(u, h, gt_m, pool_w[0], pool_scale[0])
    uf, lgt = _ffn_prep(h, norm_ffn_g[1], sh_f1, sc_f1, moe_w_router[1])
    moe = _ec_moe(uf, lgt, moe_w_gate, moe_w_up, moe_w_down, 1)
    return _residual(h, moe, gt_f1, final_g)
```
